```python
import math
import jax, jax.numpy as jnp
from jax import lax
import numpy as np

D_MODEL = 4096
BATCH = 4
SEQ = 2048
DEPTH = 2
DEC_BATCH = 8
DEC_SEQ = 4
PAST_LEN = 16384
PAGE_SIZE = 128

CHUNK = 128
D_A = 1024
G_A = 8
H_B = 8
DH = 128
D_B = H_B * 2 * DH
D_C = 1024
CONV_W = 3
N_BRANCH = 3
SPLIT_SIZES = (D_A, D_A, D_B, D_B, D_B, D_C, D_C, D_C, N_BRANCH * D_MODEL)
N_IN = sum(SPLIT_SIZES)
N_GROUPS = 4
N_EXP_PER_GROUP = 8
N_EXPERTS = N_GROUPS * N_EXP_PER_GROUP
TOP_K = 2
D_FF = 512

Q_BLOCK = 128
EPS = 1e-6

kernel_name = "hybrid_gmlp_diffattn_shortconv_hiermoe_step"


def rmsnorm(x, g):
    xf = x.astype(jnp.float32)
    r = lax.rsqrt(jnp.mean(xf * xf, axis=-1, keepdims=True) + EPS)
    return (xf * r).astype(x.dtype) * g


def alibi_slopes():
    return 2.0 ** (-8.0 * jnp.arange(1, H_B + 1, dtype=jnp.float32) / H_B)


def chunk_sgu(u, v, g_norm, w_s, b_s):
    n, t, _ = v.shape
    v = rmsnorm(v, g_norm)
    n_chunks = -(-t // CHUNK)
    pad = n_chunks * CHUNK - t
    vp = jnp.pad(v, ((0, 0), (0, pad), (0, 0))).reshape(n, n_chunks, CHUNK, G_A, D_A // G_A)
    causal = jnp.tril(jnp.ones((CHUNK, CHUNK), dtype=bool))
    w = jnp.where(causal[None], w_s, 0)
    s = jnp.einsum('gts,ncsgd->nctgd', w, vp) + b_s.T[None, None, :, :, None]
    s = s.reshape(n, n_chunks * CHUNK, D_A)[:, :t]
    start = ((t - 1) // CHUNK) * CHUNK
    return u * s, v[:, start:]


def diff_attend(q, k, v, q_pos, k_pos, lam, slopes):
    s = jnp.einsum('nqhcd,nkhcd->nhcqk', q, k).astype(jnp.float32) * (DH ** -0.5)
    dist = (q_pos[:, None] - k_pos[None, :]).astype(jnp.float32)
    s = s - slopes[None, :, None, None, None] * dist[None, None, None]
    visible = k_pos[None, :] <= q_pos[:, None]
    s = jnp.where(visible[None, None, None], s, -jnp.inf)
    p = jax.nn.softmax(s, axis=-1)
    a = p[:, :, 0] - lam * p[:, :, 1]
    return jnp.einsum('nhqk,nkhe->nqhe', a.astype(v.dtype), v)


def short_conv(bg, cg, h, w, prev):
    xc = cg * h
    xp = jnp.concatenate([prev, xc], axis=1)
    t = xc.shape[1]
    y = sum(w[j] * xp[:, j:j + t] for j in range(CONV_W))
    return bg * y, xp[:, xp.shape[1] - (CONV_W - 1):]


def mixer_sublayer(x, lp, lam, lam_init, k_past, v_past, conv_prev, q_offset):
    n, t, _ = x.shape
    xn = rmsnorm(x, lp['norm1_g'])
    z = xn @ lp['w_in']
    u_a, v_a, q, k, v_b, b_c, c_c, h_c, gates = jnp.split(z, list(np.cumsum(SPLIT_SIZES)[:-1]), axis=-1)

    a_out, a_rows = chunk_sgu(u_a, v_a, lp['sgu_norm_g'], lp['sgu_w'], lp['sgu_b'])

    q = rmsnorm(q.reshape(n, t, H_B, 2, DH), lp['q_norm_g'])
    k = rmsnorm(k.reshape(n, t, H_B, 2, DH), lp['k_norm_g'])
    v = v_b.reshape(n, t, H_B, 2 * DH)
    k_rows = k.reshape(n, t, H_B, 2 * DH)
    slopes = alibi_slopes()
    if k_past is None:
        nb = t // Q_BLOCK
        qb = q.reshape(n, nb, Q_BLOCK, H_B, 2, DH).transpose(1, 0, 2, 3, 4, 5)
        k_pos = jnp.arange(t)

        def block(args):
            qi, bi = args
            q_pos = bi * Q_BLOCK + jnp.arange(Q_BLOCK)
            return diff_attend(qi, k, v, q_pos, k_pos, lam, slopes)

        o = lax.map(block, (qb, jnp.arange(nb)))
        o = o.transpose(1, 0, 2, 3, 4).reshape(n, t, H_B, 2 * DH)
    else:
        past = k_past.shape[1]
        k_all = jnp.concatenate([k_past.reshape(n, past, H_B, 2, DH), k], axis=1)
        v_all = jnp.concatenate([v_past, v], axis=1)
        k_pos = jnp.arange(past + t)
        q_pos = q_offset + jnp.arange(t)
        o = diff_attend(q, k_all, v_all, q_pos, k_pos, lam, slopes)
    o = rmsnorm(o, lp['subln_g']) * (1.0 - lam_init)
    b_out = o.reshape(n, t, D_B)

    c_out, conv_state = short_conv(b_c, c_c, h_c, lp['conv_w'], conv_prev)

    g_a, g_b, g_c = jnp.split(jax.nn.sigmoid(gates), N_BRANCH, axis=-1)
    m = g_a * (a_out @ lp['w_br_a']) + g_b * (b_out @ lp['w_br_b']) + g_c * (c_out @ lp['w_br_c'])
    return x + m @ lp['w_out'], k_rows, v, conv_state, a_rows


def hier_moe(x, lp):
    n, t, d = x.shape
    xt = rmsnorm(x, lp['norm2_g']).reshape(n * t, d)
    g_logits = (xt @ lp['router_group_w'] + lp['router_group_b']).astype(jnp.float32)
    g_top, g_idx = lax.top_k(jax.nn.softmax(g_logits, axis=-1), 1)
    e_logits = (xt @ lp['router_expert_w'] + lp['router_expert_b']).astype(jnp.float32)
    e_logits = e_logits.reshape(-1, N_GROUPS, N_EXP_PER_GROUP)
    e_sel = jnp.take_along_axis(e_logits, g_idx[:, :, None], axis=1)[:, 0]
    e_top, e_idx = lax.top_k(jax.nn.softmax(e_sel, axis=-1), TOP_K)
    e_top = e_top / jnp.sum(e_top, axis=-1, keepdims=True)
    wts = g_top * e_top
    expert_id = g_idx * N_EXP_PER_GROUP + e_idx
    gate = jnp.sum(jax.nn.one_hot(expert_id, N_EXPERTS, dtype=jnp.float32) * wts[..., None], axis=1)
    hg = jnp.einsum('td,edf->tef', xt, lp['expert_w_gate'])
    hu = jnp.einsum('td,edf->tef', xt, lp['expert_w_up'])
    h = jax.nn.silu(hg) * hu * gate[:, :, None].astype(xt.dtype)
    y = jnp.einsum('tef,efd->td', h, lp['expert_w_down'])
    return x + y.reshape(n, t, d)


def setup_inputs(seed: int = 0) -> dict:
    key = jax.random.key(seed)
    ks = list(jax.random.split(key, 40))
    f32 = jnp.float32
    n_pages = PAST_LEN // PAGE_SIZE
    n_used = DEC_BATCH * n_pages
    n_pool = n_used + max(1, n_used // 4)

    def nrm(i, shape, scale):
        return jax.random.normal(ks[i], shape, f32) * scale

    page_table = jax.random.permutation(ks[0], n_pool)[:n_used].reshape(DEC_BATCH, n_pages).astype(jnp.int32)
    return {
        'x_prompt': nrm(1, (BATCH, SEQ, D_MODEL), 1.0),
        'x_sample': nrm(2, (DEC_BATCH, DEC_SEQ, D_MODEL), 1.0),
        'cache_k': nrm(3, (DEPTH, n_pool, PAGE_SIZE, H_B, 2 * DH), 1.0),
        'cache_v': nrm(4, (DEPTH, n_pool, PAGE_SIZE, H_B, 2 * DH), 1.0),
        'state_conv': nrm(5, (DEPTH, DEC_BATCH, CONV_W - 1, D_C), 1.0),
        'page_table': page_table,
        'norm1_g': 1.0 + nrm(6, (DEPTH, D_MODEL), 0.05),
        'w_in': nrm(7, (DEPTH, D_MODEL, N_IN), D_MODEL ** -0.5),
        'sgu_norm_g': 1.0 + nrm(8, (DEPTH, D_A), 0.05),
        'sgu_w': nrm(9, (DEPTH, G_A, CHUNK, CHUNK), 0.5 * CHUNK ** -0.5),
        'sgu_b': 1.0 + nrm(10, (DEPTH, G_A, CHUNK), 0.1),
        'q_norm_g': 1.0 + nrm(11, (DEPTH, DH), 0.05),
        'k_norm_g': 1.0 + nrm(12, (DEPTH, DH), 0.05),
        'lam_q1': nrm(13, (DEPTH, DH), 0.1),
        'lam_k1': nrm(14, (DEPTH, DH), 0.1),
        'lam_q2': nrm(15, (DEPTH, DH), 0.1),
        'lam_k2': nrm(16, (DEPTH, DH), 0.1),
        'subln_g': 1.0 + nrm(17, (DEPTH, 2 * DH), 0.05),
        'conv_w': nrm(18, (DEPTH, CONV_W, D_C), CONV_W ** -0.5),
        'w_br_a': nrm(19, (DEPTH, D_A, D_MODEL), D_A ** -0.5),
        'w_br_b': nrm(20, (DEPTH, D_B, D_MODEL), D_B ** -0.5),
        'w_br_c': nrm(21, (DEPTH, D_C, D_MODEL), D_C ** -0.5),
        'w_out': nrm(22, (DEPTH, D_MODEL, D_MODEL), D_MODEL ** -0.5),
        'norm2_g': 1.0 + nrm(23, (DEPTH, D_MODEL), 0.05),
        'router_group_w': nrm(24, (DEPTH, D_MODEL, N_GROUPS), D_MODEL ** -0.5),
        'router_group_b': nrm(25, (DEPTH, N_GROUPS), 0.01),
        'router_expert_w': nrm(26, (DEPTH, D_MODEL, N_EXPERTS), D_MODEL ** -0.5),
        'router_expert_b': nrm(27, (DEPTH, N_EXPERTS), 0.01),
        'expert_w_gate': nrm(28, (DEPTH, N_EXPERTS, D_MODEL, D_FF), D_MODEL ** -0.5),
        'expert_w_up': nrm(29, (DEPTH, N_EXPERTS, D_MODEL, D_FF), D_MODEL ** -0.5),
        'expert_w_down': nrm(30, (DEPTH, N_EXPERTS, D_FF, D_MODEL), D_FF ** -0.5),
    }


def reference(x_prompt, x_sample, cache_k, cache_v, state_conv, page_table, norm1_g, w_in,
              sgu_norm_g, sgu_w, sgu_b, q_norm_g, k_norm_g, lam_q1, lam_k1, lam_q2, lam_k2,
              subln_g, conv_w, w_br_a, w_br_b, w_br_c, w_out, norm2_g, router_group_w,
              router_group_b, router_expert_w, router_expert_b, expert_w_gate, expert_w_up,
              expert_w_down):
    n_dec, n_pages = page_table.shape
    past_len = n_pages * PAGE_SIZE
    yp, ys = x_prompt, x_sample
    kp_l, vp_l, ks_l, vs_l, cp_l, cs_l, ap_l, as_l = [], [], [], [], [], [], [], []
    for l in range(DEPTH):
        lp = {
            'norm1_g': norm1_g[l], 'w_in': w_in[l], 'sgu_norm_g': sgu_norm_g[l],
            'sgu_w': sgu_w[l], 'sgu_b': sgu_b[l], 'q_norm_g': q_norm_g[l],
            'k_norm_g': k_norm_g[l], 'subln_g': subln_g[l], 'conv_w': conv_w[l],
            'w_br_a': w_br_a[l], 'w_br_b': w_br_b[l], 'w_br_c': w_br_c[l], 'w_out': w_out[l],
            'norm2_g': norm2_g[l], 'router_group_w': router_group_w[l],
            'router_group_b': router_group_b[l], 'router_expert_w': router_expert_w[l],
            'router_expert_b': router_expert_b[l], 'expert_w_gate': expert_w_gate[l],
            'expert_w_up': expert_w_up[l], 'expert_w_down': expert_w_down[l],
        }
        lam_init = 0.8 - 0.6 * math.exp(-0.3 * l)
        lam = (jnp.exp(jnp.sum(lam_q1[l].astype(jnp.float32) * lam_k1[l].astype(jnp.float32)))
               - jnp.exp(jnp.sum(lam_q2[l].astype(jnp.float32) * lam_k2[l].astype(jnp.float32)))
               + lam_init)

        conv0 = jnp.zeros((yp.shape[0], CONV_W - 1, D_C), yp.dtype)
        yp, kp, vp, cp, ap = mixer_sublayer(yp, lp, lam, lam_init, None, None, conv0, 0)
        yp = hier_moe(yp, lp)

        k_past = cache_k[l, page_table].reshape(n_dec, past_len, H_B, 2 * DH)
        v_past = cache_v[l, page_table].reshape(n_dec, past_len, H_B, 2 * DH)
        ys, kn, vn, cn, an = mixer_sublayer(ys, lp, lam, lam_init, k_past, v_past, state_conv[l], past_len)
        ys = hier_moe(ys, lp)

        kp_l.append(kp); vp_l.append(vp); ks_l.append(kn); vs_l.append(vn)
        cp_l.append(cp); cs_l.append(cn); ap_l.append(ap); as_l.append(an)

    new_k_prompt = jnp.stack(kp_l)
    new_v_prompt = jnp.stack(vp_l)
    new_k_sample = jnp.stack(ks_l)
    new_v_sample = jnp.stack(vs_l)
    new_conv_prompt = jnp.stack(cp_l)
    new_conv_sample = jnp.stack(cs_l)
    new_chunk_v_prompt = jnp.stack(ap_l)
    new_chunk_v_sample = jnp.stack(as_l)
    return (yp, ys, new_k_prompt, new_v_prompt, new_k_sample, new_v_sample,
            new_conv_prompt, new_conv_sample, new_chunk_v_prompt, new_chunk_v_sample)
```

```python
import functools
import math

import jax
import jax.numpy as jnp
from jax import lax
from jax.experimental import pallas as pl
from jax.experimental.pallas import tpu as pltpu

F32 = jnp.float32
BF16 = jnp.bfloat16
EPS = 1e-6

CHUNK = 128
D_A = 1024
G_A = 8
H_B = 8
DH = 128
D_B = H_B * 2 * DH
D_C = 1024
CONV_W = 3
N_BRANCH = 3
N_GROUPS = 4
N_EXP_PER_GROUP = 8
N_EXPERTS = N_GROUPS * N_EXP_PER_GROUP
PAGE_SIZE = 128
LANES = 128

OFF_U, OFF_V = 0, D_A
OFF_Q = 2 * D_A
OFF_K = OFF_Q + D_B
OFF_VB = OFF_K + D_B
OFF_BG = OFF_VB + D_B
OFF_CG = OFF_BG + D_C
OFF_H = OFF_CG + D_C
OFF_GATE = OFF_H + D_C

MOE_TM = 256
MIB = 1024 * 1024


def _params(sem, vmem_mib=48):
    return pltpu.CompilerParams(dimension_semantics=sem, vmem_limit_bytes=vmem_mib * MIB)


def _dot(a, b):
    return jnp.dot(a, b, preferred_element_type=F32)


def _dot_nt(a, b):
    return lax.dot_general(a, b, (((1,), (1,)), ((), ())), preferred_element_type=F32)


def _inv_rms(x):
    return lax.rsqrt(jnp.mean(x * x, axis=-1, keepdims=True) + EPS)


def _rmsnorm_kernel(x_ref, g_ref, o_ref):
    x = x_ref[...]
    o_ref[...] = (x * _inv_rms(x) * g_ref[...]).astype(o_ref.dtype)


def rmsnorm_rows(x, g, out_dtype, tr=256):
    m, d = x.shape
    tr = min(tr, m)
    return pl.pallas_call(
        _rmsnorm_kernel,
        grid=(m // tr,),
        in_specs=[pl.BlockSpec((tr, d), lambda i: (i, 0)),
                  pl.BlockSpec((1, d), lambda i: (0, 0))],
        out_specs=pl.BlockSpec((tr, d), lambda i: (i, 0)),
        out_shape=jax.ShapeDtypeStruct((m, d), out_dtype),
        compiler_params=_params(("parallel",)),
    )(x, g.reshape(1, d))


def _mm_kernel(a_ref, w_ref, *rest, has_res):
    if has_res:
        r_ref, o_ref, wb_ref = rest
    else:
        o_ref, wb_ref = rest

    @pl.when(pl.program_id(1) == 0)
    def _():
        wb_ref[...] = w_ref[...].astype(BF16)

    acc = _dot(a_ref[...], wb_ref[...])
    if has_res:
        acc = acc + r_ref[...]
    o_ref[...] = acc.astype(o_ref.dtype)


def matmul_wcast(a, w, out_dtype, tm, tn, residual=None):
    m, k = a.shape
    n = w.shape[1]
    tm = min(tm, m)
    in_specs = [pl.BlockSpec((tm, k), lambda j, i: (i, 0)),
                pl.BlockSpec((k, tn), lambda j, i: (0, j))]
    args = [a, w]
    if residual is not None:
        in_specs.append(pl.BlockSpec((tm, tn), lambda j, i: (i, j)))
        args.append(residual)
    return pl.pallas_call(
        functools.partial(_mm_kernel, has_res=residual is not None),
        grid=(n // tn, m // tm),
        in_specs=in_specs,
        out_specs=pl.BlockSpec((tm, tn), lambda j, i: (i, j)),
        out_shape=jax.ShapeDtypeStruct((m, n), out_dtype),
        scratch_shapes=[pltpu.VMEM((k, tn), BF16)],
        compiler_params=_params(("parallel", "arbitrary"), 52),
    )(*args)


def _sgu_kernel(u_ref, v_ref, gn_ref, w_ref, bt_ref, o_ref, rows_ref):
    v = v_ref[0]
    vn = v * _inv_rms(v) * gn_ref[...]
    rows_ref[0] = vn
    vb = vn.astype(BF16)
    row = lax.broadcasted_iota(jnp.int32, (CHUNK, CHUNK), 0)
    col = lax.broadcasted_iota(jnp.int32, (CHUNK, CHUNK), 1)
    dg = D_A // G_A
    for g in range(G_A):
        wg = jnp.where(row >= col, w_ref[g], 0.0).astype(BF16)
        s = _dot(wg, vb[:, g * dg:(g + 1) * dg]) + bt_ref[:, g:g + 1]
        o_ref[0, :, g * dg:(g + 1) * dg] = (u_ref[0, :, g * dg:(g + 1) * dg] * s).astype(o_ref.dtype)


def sgu_prompt(z3, gn, w, b):
    n, t, _ = z3.shape
    nc = t // CHUNK
    return pl.pallas_call(
        _sgu_kernel,
        grid=(n, nc),
        in_specs=[pl.BlockSpec((1, CHUNK, D_A), lambda i, c: (i, c, OFF_U // D_A)),
                  pl.BlockSpec((1, CHUNK, D_A), lambda i, c: (i, c, OFF_V // D_A)),
                  pl.BlockSpec((1, D_A), lambda i, c: (0, 0)),
                  pl.BlockSpec((G_A, CHUNK, CHUNK), lambda i, c: (0, 0, 0)),
                  pl.BlockSpec((CHUNK, G_A), lambda i, c: (0, 0))],
        out_specs=[pl.BlockSpec((1, CHUNK, D_A), lambda i, c: (i, c, 0)),
                   pl.BlockSpec((1, CHUNK, D_A), lambda i, c: (i, 0, 0))],
        out_shape=[jax.ShapeDtypeStruct((n, t, D_A), BF16),
                   jax.ShapeDtypeStruct((n, CHUNK, D_A), F32)],
        compiler_params=_params(("parallel", "arbitrary")),
    )(z3, z3, gn.reshape(1, D_A), w, b.T)


def _sgu_small_kernel(u_ref, v_ref, gn_ref, wc_ref, bc_ref, o_ref, rows_ref, *, t):
    v = v_ref[...]
    vn = v * _inv_rms(v) * gn_ref[...]
    rows_ref[...] = vn
    vr = vn.astype(BF16).astype(F32)
    s = jnp.broadcast_to(bc_ref[...][None], vn.shape)
    for j in range(t):
        s = s + wc_ref[j].astype(BF16).astype(F32)[None] * vr[:, j:j + 1, :]
    o_ref[...] = (u_ref[...] * s).astype(o_ref.dtype)


def sgu_sample(z3, gn, w, b):
    n, t, _ = z3.shape
    tri = jnp.tril(jnp.ones((t, t), F32))
    w4 = w[:, :t, :t] * tri[None]
    wc = jnp.repeat(jnp.transpose(w4, (2, 1, 0)), D_A // G_A, axis=-1)
    bc = jnp.repeat(b[:, :t].T, D_A // G_A, axis=-1)
    return pl.pallas_call(
        functools.partial(_sgu_small_kernel, t=t),
        grid=(1,),
        in_specs=[pl.BlockSpec((n, t, D_A), lambda i: (0, 0, OFF_U // D_A)),
                  pl.BlockSpec((n, t, D_A), lambda i: (0, 0, OFF_V // D_A)),
                  pl.BlockSpec((1, D_A), lambda i: (0, 0)),
                  pl.BlockSpec((t, t, D_A), lambda i: (0, 0, 0)),
                  pl.BlockSpec((t, D_A), lambda i: (0, 0))],
        out_specs=[pl.BlockSpec((n, t, D_A), lambda i: (0, 0, 0)),
                   pl.BlockSpec((n, t, D_A), lambda i: (0, 0, 0))],
        out_shape=[jax.ShapeDtypeStruct((n, t, D_A), BF16),
                   jax.ShapeDtypeStruct((n, t, D_A), F32)],
        compiler_params=_params(("arbitrary",)),
    )(z3, z3, gn.reshape(1, D_A), wc, bc)


_CONV_PAD = 8


def _conv_kernel(bg_ref, cg_ref, h_ref, prev_ref, w_ref, o_ref, st_ref, xp_ref, *, tr):
    lo = _CONV_PAD - (CONV_W - 1)

    @pl.when(pl.program_id(1) == 0)
    def _():
        xp_ref[lo:_CONV_PAD, :] = prev_ref[0]

    xp_ref[_CONV_PAD:_CONV_PAD + tr, :] = cg_ref[0] * h_ref[0]
    y = w_ref[0:1, :] * xp_ref[lo:lo + tr, :]
    for j in range(1, CONV_W):
        y = y + w_ref[j:j + 1, :] * xp_ref[lo + j:lo + j + tr, :]
    o_ref[0] = (bg_ref[0] * y).astype(o_ref.dtype)
    last = xp_ref[tr + lo:tr + _CONV_PAD, :]
    st_ref[0] = last
    xp_ref[lo:_CONV_PAD, :] = last


def short_conv(z3, prev, w, tr=256):
    n, t, _ = z3.shape
    tr = min(tr, t)
    return pl.pallas_call(
        functools.partial(_conv_kernel, tr=tr),
        grid=(n, t // tr),
        in_specs=[pl.BlockSpec((1, tr, D_C), lambda i, j: (i, j, OFF_BG // D_C)),
                  pl.BlockSpec((1, tr, D_C), lambda i, j: (i, j, OFF_CG // D_C)),
                  pl.BlockSpec((1, tr, D_C), lambda i, j: (i, j, OFF_H // D_C)),
                  pl.BlockSpec((1, CONV_W - 1, D_C), lambda i, j: (i, 0, 0)),
                  pl.BlockSpec((CONV_W, D_C), lambda i, j: (0, 0))],
        out_specs=[pl.BlockSpec((1, tr, D_C), lambda i, j: (i, j, 0)),
                   pl.BlockSpec((1, CONV_W - 1, D_C), lambda i, j: (i, 0, 0))],
        out_shape=[jax.ShapeDtypeStruct((n, t, D_C), BF16),
                   jax.ShapeDtypeStruct((n, CONV_W - 1, D_C), F32)],
        scratch_shapes=[pltpu.VMEM((tr + _CONV_PAD, D_C), F32)],
        compiler_params=_params(("parallel", "arbitrary")),
    )(z3, z3, z3, prev, w)


def _qkv_kernel(q_ref, k_ref, v_ref, qg_ref, kg_ref, kf_ref, vf_ref, qb_ref, kb_ref, vb_ref):
    for g in range(D_B // DH):
        sl = slice(g * DH, (g + 1) * DH)
        q = q_ref[:, sl]
        qb_ref[:, sl] = (q * _inv_rms(q) * qg_ref[...]).astype(BF16)
        k = k_ref[:, sl]
        kn = k * _inv_rms(k) * kg_ref[...]
        kf_ref[:, sl] = kn
        kb_ref[:, sl] = kn.astype(BF16)
    v = v_ref[...]
    vf_ref[...] = v
    vb_ref[...] = v.astype(BF16)


def qkv_prep(z2, qg, kg, tr=256):
    m = z2.shape[0]
    tr = min(tr, m)
    row = lambda c: pl.BlockSpec((tr, D_B), lambda i: (i, c))
    vec = pl.BlockSpec((1, DH), lambda i: (0, 0))
    return pl.pallas_call(
        _qkv_kernel,
        grid=(m // tr,),
        in_specs=[row(OFF_Q // D_B), row(OFF_K // D_B), row(OFF_VB // D_B), vec, vec],
        out_specs=[row(0)] * 5,
        out_shape=[jax.ShapeDtypeStruct((m, D_B), F32), jax.ShapeDtypeStruct((m, D_B), F32),
                   jax.ShapeDtypeStruct((m, D_B), BF16), jax.ShapeDtypeStruct((m, D_B), BF16),
                   jax.ShapeDtypeStruct((m, D_B), BF16)],
        compiler_params=_params(("parallel",)),
    )(z2, z2, z2, qg.reshape(1, DH), kg.reshape(1, DH))


def _lam(lq1, lk1, lq2, lk2, lam_init):
    a = jnp.sum(lq1[...] * lk1[...], axis=-1, keepdims=True)
    b = jnp.sum(lq2[...] * lk2[...], axis=-1, keepdims=True)
    return jnp.exp(a) - jnp.exp(b) + lam_init


def _softmax_stats(s, c, m_ref, l_ref):
    m_prev = m_ref[c]
    m_new = jnp.maximum(m_prev, jnp.max(s, axis=-1, keepdims=True))
    l_ref[c] = l_ref[c] * jnp.exp(m_prev - m_new) + jnp.sum(jnp.exp(s - m_new), axis=-1, keepdims=True)
    m_ref[c] = m_new


def _diff_weights(s0, s1, lam, m_ref, l_ref):
    p0 = jnp.exp(s0 - m_ref[0]) * (1.0 / l_ref[0])
    p1 = jnp.exp(s1 - m_ref[1]) * (1.0 / l_ref[1])
    return (p0 - lam * p1).astype(BF16)


def _subln(o, lam_init, g_ref):
    return (o * _inv_rms(o) * g_ref[...]) * (1.0 - lam_init)


def _attn_kernel(slope_ref, q_ref, k_ref, v_ref, lq1, lk1, lq2, lk2, g_ref, o_ref,
                 m_ref, l_ref, acc_ref, *, tq, tk, lam_init):
    h = pl.program_id(1)
    qi = pl.program_id(2)
    slope = slope_ref[h]
    m_ref[...] = jnp.full(m_ref.shape, -jnp.inf, F32)
    l_ref[...] = jnp.zeros(l_ref.shape, F32)
    acc_ref[...] = jnp.zeros(acc_ref.shape, F32)
    q = q_ref[0]
    rel = (lax.broadcasted_iota(jnp.int32, (tq, tk), 0)
           - lax.broadcasted_iota(jnp.int32, (tq, tk), 1))
    scale = DH ** -0.5
    n_kv = (qi * tq + tq + tk - 1) // tk

    def scores(j):
        k = k_ref[0, pl.ds(pl.multiple_of(j * tk, tk), tk), :]
        dist = rel + (qi * tq - j * tk)
        bias = slope * dist.astype(F32)
        out = []
        for c in range(2):
            s = _dot_nt(q[:, c * DH:(c + 1) * DH], k[:, c * DH:(c + 1) * DH]) * scale - bias
            out.append(jnp.where(dist >= 0, s, -jnp.inf))
        return out

    def stats_body(j, carry):
        for c, s in enumerate(scores(j)):
            _softmax_stats(s, c, m_ref, l_ref)
        return carry

    lax.fori_loop(0, n_kv, stats_body, 0)
    lam = _lam(lq1, lk1, lq2, lk2, lam_init)

    def pv_body(j, carry):
        s0, s1 = scores(j)
        a = _diff_weights(s0, s1, lam, m_ref, l_ref)
        v = v_ref[0, pl.ds(pl.multiple_of(j * tk, tk), tk), :]
        acc_ref[...] += _dot(a, v)
        return carry

    lax.fori_loop(0, n_kv, pv_body, 0)
    o_ref[0] = _subln(acc_ref[...], lam_init, g_ref).astype(o_ref.dtype)


def _alibi_slopes():
    return 2.0 ** (-8.0 * jnp.arange(1, H_B + 1, dtype=F32) / H_B)


def attn_prompt(qb, kb, vb, lam_vecs, subln_g, lam_init, tq=256, tk=256):
    n, t, _ = qb.shape
    tq = min(tq, t)
    tk = min(tk, t)
    vec = pl.BlockSpec((1, DH), lambda i, h, j: (0, 0))
    return pl.pallas_call(
        functools.partial(_attn_kernel, tq=tq, tk=tk, lam_init=lam_init),
        grid=(n, H_B, t // tq),
        in_specs=[pl.BlockSpec(memory_space=pltpu.SMEM),
                  pl.BlockSpec((1, tq, 2 * DH), lambda i, h, j: (i, j, h)),
                  pl.BlockSpec((1, t, 2 * DH), lambda i, h, j: (i, 0, h)),
                  pl.BlockSpec((1, t, 2 * DH), lambda i, h, j: (i, 0, h)),
                  vec, vec, vec, vec,
                  pl.BlockSpec((1, 2 * DH), lambda i, h, j: (0, 0))],
        out_specs=pl.BlockSpec((1, tq, 2 * DH), lambda i, h, j: (i, j, h)),
        out_shape=jax.ShapeDtypeStruct((n, t, D_B), BF16),
        scratch_shapes=[pltpu.VMEM((2, tq, 1), F32), pltpu.VMEM((2, tq, 1), F32),
                        pltpu.VMEM((tq, 2 * DH), F32)],
        compiler_params=_params(("parallel", "parallel", "arbitrary")),
    )(_alibi_slopes(), qb, kb, vb, *[x.reshape(1, DH) for x in lam_vecs], subln_g.reshape(1, 2 * DH))


def _decode_kernel(pt_ref, qm_ref, k_ref, v_ref, base_ref, slope_ref, kn_ref, vn_ref, basen_ref,
                   lq1, lk1, lq2, lk2, g_ref, o_ref, m_ref, l_ref, acc_ref, *, nrow, lam_init):
    ph = pl.program_id(1)
    p = pl.program_id(2)
    last = pl.num_programs(2) - 1
    scale = DH ** -0.5
    qm = qm_ref[0]
    k = k_ref[0, 0].astype(BF16)
    shift = slope_ref[...] * (p * PAGE_SIZE).astype(F32)
    s = _dot_nt(qm, k) * scale + (base_ref[...] + shift)

    def new_scores():
        return _dot_nt(qm, kn_ref[0]) * scale + basen_ref[...]

    @pl.when((ph == 0) & (p == 0))
    def _():
        m_ref[...] = jnp.full(m_ref.shape, -jnp.inf, F32)
        l_ref[...] = jnp.zeros(l_ref.shape, F32)

    @pl.when(ph == 0)
    def _():
        for c in range(2):
            _softmax_stats(s[c * nrow:(c + 1) * nrow], c, m_ref, l_ref)

        @pl.when(p == last)
        def _():
            sn = new_scores()
            for c in range(2):
                _softmax_stats(sn[c * nrow:(c + 1) * nrow], c, m_ref, l_ref)

    @pl.when((ph == 1) & (p == 0))
    def _():
        acc_ref[...] = jnp.zeros(acc_ref.shape, F32)

    @pl.when(ph == 1)
    def _():
        lam = _lam(lq1, lk1, lq2, lk2, lam_init)
        a = _diff_weights(s[:nrow], s[nrow:], lam, m_ref, l_ref)
        acc_ref[...] += _dot(a, v_ref[0, 0].astype(BF16))

        @pl.when(p == last)
        def _():
            sn = new_scores()
            an = _diff_weights(sn[:nrow], sn[nrow:], lam, m_ref, l_ref)
            o = acc_ref[...] + _dot(an, vn_ref[0])
            o_ref[0] = _subln(o, lam_init, g_ref).astype(o_ref.dtype)


def attn_decode(layer, qb, kb, vb, cache_k4, cache_v4, page_table, lam_vecs, subln_g, lam_init, t):
    nb, n_pages = page_table.shape
    past = n_pages * PAGE_SIZE
    nrow = t * H_B
    ncol = PAGE_SIZE * H_B
    q5 = qb.reshape(nb, t, H_B, 2, DH)
    zero = jnp.zeros_like(q5[:, :, :, 0])
    qm = jnp.stack([jnp.concatenate([q5[:, :, :, 0], zero], axis=-1),
                    jnp.concatenate([zero, q5[:, :, :, 1]], axis=-1)], axis=1)
    qm = qm.reshape(nb, 2 * nrow, 2 * DH)
    kn = kb.reshape(nb, nrow, 2 * DH)
    vn = vb.reshape(nb, nrow, 2 * DH)
    slopes = _alibi_slopes()
    r = jnp.arange(2 * nrow)
    r_tok = (r % nrow) // H_B
    r_head = r % H_B
    c = jnp.arange(ncol)
    c_pos, c_head = c // H_B, c % H_B
    r_slope = slopes[r_head]
    base = -r_slope[:, None] * (past + r_tok[:, None] - c_pos[None, :]).astype(F32)
    base = jnp.where(r_head[:, None] == c_head[None, :], base, -jnp.inf)
    cn = jnp.arange(nrow)
    cn_tok, cn_head = cn // H_B, cn % H_B
    basen = -r_slope[:, None] * (r_tok[:, None] - cn_tok[None, :]).astype(F32)
    basen = jnp.where((r_head[:, None] == cn_head[None, :]) & (cn_tok[None, :] <= r_tok[:, None]),
                      basen, -jnp.inf)
    vec = pl.BlockSpec((1, DH), lambda b, ph, p, pt: (0, 0))
    const2 = lambda shape: pl.BlockSpec(shape, lambda b, ph, p, pt: (0, 0))
    per_b = lambda shape: pl.BlockSpec(shape, lambda b, ph, p, pt: (b, 0, 0))
    k_page = pl.BlockSpec((1, 1, ncol, 2 * DH), lambda b, ph, p, pt: (layer, pt[b, p], 0, 0))
    v_page = pl.BlockSpec((1, 1, ncol, 2 * DH), lambda b, ph, p, pt: (layer, pt[b, p * ph], 0, 0))
    grid_spec = pltpu.PrefetchScalarGridSpec(
        num_scalar_prefetch=1,
        grid=(nb, 2, n_pages),
        in_specs=[per_b((1, 2 * nrow, 2 * DH)), k_page, v_page,
                  const2((2 * nrow, ncol)), const2((2 * nrow, 1)),
                  per_b((1, nrow, 2 * DH)), per_b((1, nrow, 2 * DH)), const2((2 * nrow, nrow)),
                  vec, vec, vec, vec, const2((1, 2 * DH))],
        out_specs=per_b((1, nrow, 2 * DH)),
        scratch_shapes=[pltpu.VMEM((2, nrow, 1), F32), pltpu.VMEM((2, nrow, 1), F32),
                        pltpu.VMEM((nrow, 2 * DH), F32)],
    )
    out = pl.pallas_call(
        functools.partial(_decode_kernel, nrow=nrow, lam_init=lam_init),
        grid_spec=grid_spec,
        out_shape=jax.ShapeDtypeStruct((nb, nrow, 2 * DH), BF16),
        compiler_params=_params(("parallel", "arbitrary", "arbitrary")),
    )(page_table, qm, cache_k4, cache_v4, base, r_slope[:, None], kn, vn, basen,
      *[x.reshape(1, DH) for x in lam_vecs], subln_g.reshape(1, 2 * DH))
    return out.reshape(nb * t, D_B)


def _merge_kernel(a_ref, b_ref, c_ref, wa_ref, wb_ref, wc_ref, ga_ref, gb_ref, gc_ref, o_ref,
                  wab_ref, wbb_ref, wcb_ref):
    @pl.when(pl.program_id(1) == 0)
    def _():
        wab_ref[...] = wa_ref[...].astype(BF16)
        wbb_ref[...] = wb_ref[...].astype(BF16)
        wcb_ref[...] = wc_ref[...].astype(BF16)

    m = jax.nn.sigmoid(ga_ref[...]) * _dot(a_ref[...], wab_ref[...])
    m = m + jax.nn.sigmoid(gb_ref[...]) * _dot(b_ref[...], wbb_ref[...])
    m = m + jax.nn.sigmoid(gc_ref[...]) * _dot(c_ref[...], wcb_ref[...])
    o_ref[...] = m.astype(o_ref.dtype)


def merge_branches(a, b, c, wa, wb, wc, z2, tm=512, tn=512):
    m = a.shape[0]
    d = wa.shape[1]
    tm = min(tm, m)
    act = lambda k: pl.BlockSpec((tm, k), lambda j, i: (i, 0))
    wgt = lambda k: pl.BlockSpec((k, tn), lambda j, i: (0, j))
    gate = lambda br: pl.BlockSpec((tm, tn), lambda j, i: (i, (OFF_GATE + br * d) // tn + j))
    return pl.pallas_call(
        _merge_kernel,
        grid=(d // tn, m // tm),
        in_specs=[act(D_A), act(D_B), act(D_C), wgt(D_A), wgt(D_B), wgt(D_C),
                  gate(0), gate(1), gate(2)],
        out_specs=pl.BlockSpec((tm, tn), lambda j, i: (i, j)),
        out_shape=jax.ShapeDtypeStruct((m, d), BF16),
        scratch_shapes=[pltpu.VMEM((D_A, tn), BF16), pltpu.VMEM((D_B, tn), BF16),
                        pltpu.VMEM((D_C, tn), BF16)],
        compiler_params=_params(("parallel", "arbitrary"), 52),
    )(a, b, c, wa, wb, wc, z2, z2, z2)


def _router_kernel(x_ref, w_ref, b_ref, o_ref):
    logit = _dot(x_ref[...].astype(BF16), w_ref[...].astype(BF16)) + b_ref[...]
    lane = lax.broadcasted_iota(jnp.int32, logit.shape, 1).astype(F32)
    big = float(LANES)
    ninf = -jnp.inf

    def first_max(x):
        mx = jnp.max(x, axis=-1, keepdims=True)
        return mx, jnp.min(jnp.where(x == mx, lane, big), axis=-1, keepdims=True)

    gl = jnp.where(lane < N_GROUPS, logit, ninf)
    gmax, gidx = first_max(gl)
    g_top = 1.0 / jnp.sum(jnp.exp(gl - gmax), axis=-1, keepdims=True)
    lo = N_GROUPS + N_EXP_PER_GROUP * gidx
    el = jnp.where((lane >= lo) & (lane < lo + N_EXP_PER_GROUP), logit, ninf)
    l1, i1 = first_max(el)
    l2, i2 = first_max(jnp.where(lane == i1, ninf, el))
    e2 = jnp.exp(l2 - l1)
    den = 1.0 + e2
    w1 = g_top / den
    w2 = g_top * e2 / den
    out = jnp.where(lane == 0, i1 - N_GROUPS, 0.0)
    out = jnp.where(lane == 1, i2 - N_GROUPS, out)
    out = jnp.where(lane == 2, w1, out)
    out = jnp.where(lane == 3, w2, out)
    o_ref[...] = out


def router(xt, wg, bg, we, be, tr=256):
    m, d = xt.shape
    tr = min(tr, m)
    pad = LANES - N_GROUPS - N_EXPERTS
    w = jnp.concatenate([wg, we, jnp.zeros((d, pad), F32)], axis=1)
    b = jnp.concatenate([bg, be, jnp.zeros((pad,), F32)]).reshape(1, LANES)
    return pl.pallas_call(
        _router_kernel,
        grid=(m // tr,),
        in_specs=[pl.BlockSpec((tr, d), lambda i: (i, 0)),
                  pl.BlockSpec((d, LANES), lambda i: (0, 0)),
                  pl.BlockSpec((1, LANES), lambda i: (0, 0))],
        out_specs=pl.BlockSpec((tr, LANES), lambda i: (i, 0)),
        out_shape=jax.ShapeDtypeStruct((m, LANES), F32),
        compiler_params=_params(("parallel",)),
    )(xt, w, b)


def _moe_kernel(te_ref, nv_ref, tok_ref, dst_ref, x_hbm, rw_ref, wg_ref, wu_ref, wd_ref, o_hbm,
                xbuf, obuf, gsem, ssem, *, tm):
    i = pl.program_id(0)
    nt = pl.num_programs(0)
    slot = i % 2

    def row_copy_in(tile, sl, r):
        tok = tok_ref[tile * tm + r]
        return pltpu.make_async_copy(x_hbm.at[pl.ds(tok, 1)], xbuf.at[sl, pl.ds(r, 1)], gsem.at[sl])

    def row_copy_out(tile, sl, r):
        dst = dst_ref[tile * tm + r]
        return pltpu.make_async_copy(obuf.at[sl, pl.ds(r, 1)], o_hbm.at[pl.ds(dst, 1)], ssem.at[sl])

    def for_rows(tile, fn):
        def body(r, carry):
            fn(r)
            return carry
        lax.fori_loop(0, nv_ref[tile], body, 0)

    @pl.when(i == 0)
    def _():
        xbuf[...] = jnp.zeros(xbuf.shape, xbuf.dtype)
        for_rows(0, lambda r: row_copy_in(0, 0, r).start())

    @pl.when(i + 1 < nt)
    def _():
        for_rows(i + 1, lambda r: row_copy_in(i + 1, 1 - slot, r).start())

    @pl.when(i >= 2)
    def _():
        for_rows(i - 2, lambda r: row_copy_out(i - 2, slot, r).wait())

    for_rows(i, lambda r: row_copy_in(i, slot, r).wait())

    @pl.when(nv_ref[i] > 0)
    def _():
        x = xbuf[slot].astype(BF16)
        hg = _dot(x, wg_ref[0])
        hu = _dot(x, wu_ref[0])
        hmid = (hg * jax.nn.sigmoid(hg)) * hu * rw_ref[...]
        obuf[slot] = _dot(hmid.astype(BF16), wd_ref[0])

    for_rows(i, lambda r: row_copy_out(i, slot, r).start())

    @pl.when(i == nt - 1)
    def _():
        @pl.when(i >= 1)
        def _():
            for_rows(i - 1, lambda r: row_copy_out(i - 1, 1 - slot, r).wait())
        for_rows(i, lambda r: row_copy_out(i, slot, r).wait())


def moe_experts(xt, ids, wts, wg, wu, wd, tm=MOE_TM):
    m, d = xt.shape
    n_pairs = 2 * m
    nt = -(-n_pairs // tm) + N_EXPERTS
    flat_e = ids.reshape(-1)
    order = jnp.argsort(flat_e, stable=True).astype(jnp.int32)
    sorted_e = flat_e[order]
    counts = jnp.sum(flat_e[:, None] == jnp.arange(N_EXPERTS, dtype=jnp.int32)[None, :],
                     axis=0, dtype=jnp.int32)
    tiles_e = (counts + tm - 1) // tm
    tile_end = jnp.cumsum(tiles_e)
    tile_start = tile_end - tiles_e
    pair_start = jnp.cumsum(counts) - counts
    slot = tile_start[sorted_e] * tm + (jnp.arange(n_pairs, dtype=jnp.int32) - pair_start[sorted_e])
    row_tok = jnp.zeros((nt * tm,), jnp.int32).at[slot].set(order // 2)
    row_dst = jnp.zeros((nt * tm,), jnp.int32).at[slot].set((order % 2) * m + order // 2)
    row_w = jnp.zeros((nt * tm,), F32).at[slot].set(wts.reshape(-1)[order]).reshape(nt * tm, 1)
    tile_idx = jnp.arange(nt, dtype=jnp.int32)
    tile_e = jnp.minimum(jnp.searchsorted(tile_end, tile_idx, side="right"), N_EXPERTS - 1).astype(jnp.int32)
    n_valid = jnp.clip(counts[tile_e] - (tile_idx - tile_start[tile_e]) * tm, 0, tm)
    n_valid = jnp.where(tile_idx < tile_end[-1], n_valid, 0).astype(jnp.int32)
    dff = wg.shape[-1]
    grid_spec = pltpu.PrefetchScalarGridSpec(
        num_scalar_prefetch=4,
        grid=(nt,),
        in_specs=[pl.BlockSpec(memory_space=pl.ANY),
                  pl.BlockSpec((tm, 1), lambda i, te, nv, tk, ds: (i, 0)),
                  pl.BlockSpec((1, d, dff), lambda i, te, nv, tk, ds: (te[i], 0, 0)),
                  pl.BlockSpec((1, d, dff), lambda i, te, nv, tk, ds: (te[i], 0, 0)),
                  pl.BlockSpec((1, dff, d), lambda i, te, nv, tk, ds: (te[i], 0, 0))],
        out_specs=pl.BlockSpec(memory_space=pl.ANY),
        scratch_shapes=[pltpu.VMEM((2, tm, d), F32), pltpu.VMEM((2, tm, d), F32),
                        pltpu.SemaphoreType.DMA((2,)), pltpu.SemaphoreType.DMA((2,))],
    )
    out = pl.pallas_call(
        functools.partial(_moe_kernel, tm=tm),
        grid_spec=grid_spec,
        out_shape=jax.ShapeDtypeStruct((2 * m, d), F32),
        compiler_params=_params(("arbitrary",), 56),
    )(tile_e, n_valid, row_tok, row_dst, xt, row_w, wg, wu, wd)
    return out.reshape(2, m, d)


def _combine_kernel(h_ref, o0_ref, o1_ref, y_ref):
    y_ref[...] = h_ref[...] + (o0_ref[0] + o1_ref[0])


def moe_combine(h, o2, row0, tr=256):
    m, d = h.shape
    tr = min(tr, m)
    return pl.pallas_call(
        _combine_kernel,
        grid=(m // tr,),
        in_specs=[pl.BlockSpec((tr, d), lambda i: (i, 0)),
                  pl.BlockSpec((1, tr, d), lambda i: (0, row0 // tr + i, 0)),
                  pl.BlockSpec((1, tr, d), lambda i: (1, row0 // tr + i, 0))],
        out_specs=pl.BlockSpec((tr, d), lambda i: (i, 0)),
        out_shape=jax.ShapeDtypeStruct((m, d), F32),
        compiler_params=_params(("parallel",)),
    )(h, o2, o2)


def kernel(x_prompt, x_sample, cache_k, cache_v, state_conv, page_table, norm1_g, w_in, sgu_norm_g, sgu_w, sgu_b, q_norm_g, k_norm_g, lam_q1, lam_k1, lam_q2, lam_k2, subln_g, conv_w, w_br_a, w_br_b, w_br_c, w_out, norm2_g, router_group_w, router_group_b, router_expert_w, router_expert_b, expert_w_gate, expert_w_up, expert_w_down):
    depth = w_in.shape[0]
    n_p, t_p, d = x_prompt.shape
    n_s, t_s, _ = x_sample.shape
    m_p, m_s = n_p * t_p, n_s * t_s
    n_in = w_in.shape[-1]
    pool = cache_k.shape[1]
    cache_k4 = cache_k.reshape(depth, pool, PAGE_SIZE * H_B, 2 * DH)
    cache_v4 = cache_v.reshape(depth, pool, PAGE_SIZE * H_B, 2 * DH)

    xp = x_prompt.reshape(m_p, d)
    xs = x_sample.reshape(m_s, d)
    outs = {k: [] for k in ("kp", "vp", "ks", "vs", "cp", "cs", "ap", "as")}
    for l in range(depth):
        lam_init = 0.8 - 0.6 * math.exp(-0.3 * l)
        lam_vecs = (lam_q1[l], lam_k1[l], lam_q2[l], lam_k2[l])

        def trunk(x2, mixers, tm):
            xn = rmsnorm_rows(x2, norm1_g[l], BF16)
            z2 = matmul_wcast(xn, w_in[l], F32, tm=tm, tn=512)
            a_out, b_out, c_out, extras = mixers(z2)
            mm = merge_branches(a_out, b_out, c_out, w_br_a[l], w_br_b[l], w_br_c[l], z2)
            hh = matmul_wcast(mm, w_out[l], F32, tm=tm, tn=512, residual=x2)
            xt = rmsnorm_rows(hh, norm2_g[l], F32)
            route = router(xt, router_group_w[l], router_group_b[l],
                           router_expert_w[l], router_expert_b[l])
            return hh, xt, route, extras

        def prompt_mixers(z2):
            z3 = z2.reshape(n_p, t_p, n_in)
            a_out, a_rows = sgu_prompt(z3, sgu_norm_g[l], sgu_w[l], sgu_b[l])
            c_out, c_state = short_conv(z3, jnp.zeros((n_p, CONV_W - 1, D_C), F32), conv_w[l])
            kf, vf, qb, kb, vb = qkv_prep(z2, q_norm_g[l], k_norm_g[l])
            b_out = attn_prompt(qb.reshape(n_p, t_p, D_B), kb.reshape(n_p, t_p, D_B),
                                vb.reshape(n_p, t_p, D_B), lam_vecs, subln_g[l], lam_init)
            return (a_out.reshape(m_p, D_A), b_out.reshape(m_p, D_B), c_out.reshape(m_p, D_C),
                    (kf.reshape(n_p, t_p, H_B, 2 * DH), vf.reshape(n_p, t_p, H_B, 2 * DH), c_state, a_rows))

        def sample_mixers(z2):
            z3 = z2.reshape(n_s, t_s, n_in)
            a_out, a_rows = sgu_sample(z3, sgu_norm_g[l], sgu_w[l], sgu_b[l])
            c_out, c_state = short_conv(z3, state_conv[l], conv_w[l])
            kf, vf, qb, kb, vb = qkv_prep(z2, q_norm_g[l], k_norm_g[l])
            b_out = attn_decode(l, qb, kb, vb, cache_k4, cache_v4, page_table, lam_vecs,
                                subln_g[l], lam_init, t_s)
            return (a_out.reshape(m_s, D_A), b_out, c_out.reshape(m_s, D_C),
                    (kf.reshape(n_s, t_s, H_B, 2 * DH), vf.reshape(n_s, t_s, H_B, 2 * DH), c_state, a_rows))

        h_p, xt_p, route_p, ex_p = trunk(xp, prompt_mixers, 1024)
        h_s, xt_s, route_s, ex_s = trunk(xs, sample_mixers, 1024)

        xt_all = jnp.concatenate([xt_p, xt_s], axis=0)
        route = jnp.concatenate([route_p[:, :4], route_s[:, :4]], axis=0)
        o2 = moe_experts(xt_all, route[:, :2].astype(jnp.int32), route[:, 2:4], expert_w_gate[l].astype(BF16),
                         expert_w_up[l].astype(BF16), expert_w_down[l].astype(BF16))
        xp = moe_combine(h_p, o2, 0)
        xs = moe_combine(h_s, o2, m_p)

        for key, val in zip(("kp", "vp", "cp", "ap"), ex_p):
            outs[key].append(val)
        for key, val in zip(("ks", "vs", "cs", "as"), ex_s):
            outs[key].append(val)

    st = lambda key: jnp.stack(outs[key])
    return (xp.reshape(n_p, t_p, d), xs.reshape(n_s, t_s, d), st("kp"), st("vp"), st("ks"), st("vs"),
            st("cp"), st("cs"), st("ap"), st("as"))
```

```python
import functools
import math

import jax
import jax.numpy as jnp
from jax import lax
from jax.experimental import pallas as pl
from jax.experimental.pallas import tpu as pltpu

F32 = jnp.float32
BF16 = jnp.bfloat16
EPS = 1e-6

CHUNK = 128
D_A = 1024
G_A = 8
H_B = 8
DH = 128
D_B = H_B * 2 * DH
D_C = 1024
CONV_W = 3
N_BRANCH = 3
N_GROUPS = 4
N_EXP_PER_GROUP = 8
N_EXPERTS = N_GROUPS * N_EXP_PER_GROUP
PAGE_SIZE = 128
LANES = 128

OFF_U, OFF_V = 0, D_A
OFF_Q = 2 * D_A
OFF_K = OFF_Q + D_B
OFF_VB = OFF_K + D_B
OFF_BG = OFF_VB + D_B
OFF_CG = OFF_BG + D_C
OFF_H = OFF_CG + D_C
OFF_GATE = OFF_H + D_C

LOG2E = math.log2(math.e)
MOE_TM = 256
DECODE_PAGES = 4
MIB = 1024 * 1024


def _params(sem, vmem_mib=48):
    return pltpu.CompilerParams(dimension_semantics=sem, vmem_limit_bytes=vmem_mib * MIB)


def _dot(a, b):
    return jnp.dot(a, b, preferred_element_type=F32)


def _dot_nt(a, b):
    return lax.dot_general(a, b, (((1,), (1,)), ((), ())), preferred_element_type=F32)


def _inv_rms(x):
    return lax.rsqrt(jnp.mean(x * x, axis=-1, keepdims=True) + EPS)


def _rmsnorm_kernel(x_ref, g_ref, o_ref):
    x = x_ref[...]
    o_ref[...] = (x * _inv_rms(x) * g_ref[...]).astype(o_ref.dtype)


def rmsnorm_rows(x, g, out_dtype, tr=256):
    m, d = x.shape
    tr = min(tr, m)
    return pl.pallas_call(
        _rmsnorm_kernel,
        grid=(m // tr,),
        in_specs=[pl.BlockSpec((tr, d), lambda i: (i, 0)),
                  pl.BlockSpec((1, d), lambda i: (0, 0))],
        out_specs=pl.BlockSpec((tr, d), lambda i: (i, 0)),
        out_shape=jax.ShapeDtypeStruct((m, d), out_dtype),
        compiler_params=_params(("parallel",)),
    )(x, g.reshape(1, d))


def _mm_kernel(a_ref, w_ref, *rest, has_res):
    if has_res:
        r_ref, o_ref, wb_ref = rest
    else:
        o_ref, wb_ref = rest

    @pl.when(pl.program_id(1) == 0)
    def _():
        wb_ref[...] = w_ref[0].astype(BF16)

    acc = _dot(a_ref[...], wb_ref[...])
    if has_res:
        acc = acc + r_ref[...]
    o_ref[...] = acc.astype(o_ref.dtype)


def matmul_wcast(a, w, layer, out_dtype, tm, tn, residual=None):
    m, k = a.shape
    n = w.shape[2]
    tm = min(tm, m)
    in_specs = [pl.BlockSpec((tm, k), lambda j, i: (i, 0)),
                pl.BlockSpec((1, k, tn), lambda j, i: (layer, 0, j))]
    args = [a, w]
    if residual is not None:
        in_specs.append(pl.BlockSpec((tm, tn), lambda j, i: (i, j)))
        args.append(residual)
    return pl.pallas_call(
        functools.partial(_mm_kernel, has_res=residual is not None),
        grid=(n // tn, m // tm),
        in_specs=in_specs,
        out_specs=pl.BlockSpec((tm, tn), lambda j, i: (i, j)),
        out_shape=jax.ShapeDtypeStruct((m, n), out_dtype),
        scratch_shapes=[pltpu.VMEM((k, tn), BF16)],
        compiler_params=_params(("parallel", "arbitrary"), 52),
    )(*args)


def _sgu_kernel(u_ref, v_ref, gn_ref, w_ref, bt_ref, o_ref, rows_ref):
    v = v_ref[0]
    vn = v * _inv_rms(v) * gn_ref[...]
    rows_ref[0] = vn
    vb = vn.astype(BF16)
    row = lax.broadcasted_iota(jnp.int32, (CHUNK, CHUNK), 0)
    col = lax.broadcasted_iota(jnp.int32, (CHUNK, CHUNK), 1)
    dg = D_A // G_A
    for g in range(G_A):
        wg = jnp.where(row >= col, w_ref[g], 0.0).astype(BF16)
        s = _dot(wg, vb[:, g * dg:(g + 1) * dg]) + bt_ref[:, g:g + 1]
        o_ref[0, :, g * dg:(g + 1) * dg] = (u_ref[0, :, g * dg:(g + 1) * dg] * s).astype(o_ref.dtype)


def sgu_prompt(z3, gn, w, b):
    n, t, _ = z3.shape
    nc = t // CHUNK
    return pl.pallas_call(
        _sgu_kernel,
        grid=(n, nc),
        in_specs=[pl.BlockSpec((1, CHUNK, D_A), lambda i, c: (i, c, OFF_U // D_A)),
                  pl.BlockSpec((1, CHUNK, D_A), lambda i, c: (i, c, OFF_V // D_A)),
                  pl.BlockSpec((1, D_A), lambda i, c: (0, 0)),
                  pl.BlockSpec((G_A, CHUNK, CHUNK), lambda i, c: (0, 0, 0)),
                  pl.BlockSpec((CHUNK, G_A), lambda i, c: (0, 0))],
        out_specs=[pl.BlockSpec((1, CHUNK, D_A), lambda i, c: (i, c, 0)),
                   pl.BlockSpec((1, CHUNK, D_A), lambda i, c: (i, 0, 0))],
        out_shape=[jax.ShapeDtypeStruct((n, t, D_A), BF16),
                   jax.ShapeDtypeStruct((n, CHUNK, D_A), F32)],
        compiler_params=_params(("parallel", "arbitrary")),
    )(z3, z3, gn.reshape(1, D_A), w, b.T)


def _sgu_small_kernel(u_ref, v_ref, gn_ref, wc_ref, bc_ref, o_ref, rows_ref, *, t):
    v = v_ref[...]
    vn = v * _inv_rms(v) * gn_ref[...]
    rows_ref[...] = vn
    vr = vn.astype(BF16).astype(F32)
    s = jnp.broadcast_to(bc_ref[...][None], vn.shape)
    for j in range(t):
        s = s + wc_ref[j].astype(BF16).astype(F32)[None] * vr[:, j:j + 1, :]
    o_ref[...] = (u_ref[...] * s).astype(o_ref.dtype)


def sgu_sample(z3, gn, w, b):
    n, t, _ = z3.shape
    tri = jnp.tril(jnp.ones((t, t), F32))
    w4 = w[:, :t, :t] * tri[None]
    wc = jnp.repeat(jnp.transpose(w4, (2, 1, 0)), D_A // G_A, axis=-1)
    bc = jnp.repeat(b[:, :t].T, D_A // G_A, axis=-1)
    return pl.pallas_call(
        functools.partial(_sgu_small_kernel, t=t),
        grid=(1,),
        in_specs=[pl.BlockSpec((n, t, D_A), lambda i: (0, 0, OFF_U // D_A)),
                  pl.BlockSpec((n, t, D_A), lambda i: (0, 0, OFF_V // D_A)),
                  pl.BlockSpec((1, D_A), lambda i: (0, 0)),
                  pl.BlockSpec((t, t, D_A), lambda i: (0, 0, 0)),
                  pl.BlockSpec((t, D_A), lambda i: (0, 0))],
        out_specs=[pl.BlockSpec((n, t, D_A), lambda i: (0, 0, 0)),
                   pl.BlockSpec((n, t, D_A), lambda i: (0, 0, 0))],
        out_shape=[jax.ShapeDtypeStruct((n, t, D_A), BF16),
                   jax.ShapeDtypeStruct((n, t, D_A), F32)],
        compiler_params=_params(("arbitrary",)),
    )(z3, z3, gn.reshape(1, D_A), wc, bc)


_CONV_PAD = 8


def _conv_kernel(bg_ref, cg_ref, h_ref, prev_ref, w_ref, o_ref, st_ref, xp_ref, *, tr):
    lo = _CONV_PAD - (CONV_W - 1)

    @pl.when(pl.program_id(1) == 0)
    def _():
        xp_ref[lo:_CONV_PAD, :] = prev_ref[0]

    xp_ref[_CONV_PAD:_CONV_PAD + tr, :] = cg_ref[0] * h_ref[0]
    y = w_ref[0:1, :] * xp_ref[lo:lo + tr, :]
    for j in range(1, CONV_W):
        y = y + w_ref[j:j + 1, :] * xp_ref[lo + j:lo + j + tr, :]
    o_ref[0] = (bg_ref[0] * y).astype(o_ref.dtype)
    last = xp_ref[tr + lo:tr + _CONV_PAD, :]
    st_ref[0] = last
    xp_ref[lo:_CONV_PAD, :] = last


def short_conv(z3, prev, w, tr=256):
    n, t, _ = z3.shape
    tr = min(tr, t)
    return pl.pallas_call(
        functools.partial(_conv_kernel, tr=tr),
        grid=(n, t // tr),
        in_specs=[pl.BlockSpec((1, tr, D_C), lambda i, j: (i, j, OFF_BG // D_C)),
                  pl.BlockSpec((1, tr, D_C), lambda i, j: (i, j, OFF_CG // D_C)),
                  pl.BlockSpec((1, tr, D_C), lambda i, j: (i, j, OFF_H // D_C)),
                  pl.BlockSpec((1, CONV_W - 1, D_C), lambda i, j: (i, 0, 0)),
                  pl.BlockSpec((CONV_W, D_C), lambda i, j: (0, 0))],
        out_specs=[pl.BlockSpec((1, tr, D_C), lambda i, j: (i, j, 0)),
                   pl.BlockSpec((1, CONV_W - 1, D_C), lambda i, j: (i, 0, 0))],
        out_shape=[jax.ShapeDtypeStruct((n, t, D_C), BF16),
                   jax.ShapeDtypeStruct((n, CONV_W - 1, D_C), F32)],
        scratch_shapes=[pltpu.VMEM((tr + _CONV_PAD, D_C), F32)],
        compiler_params=_params(("parallel", "arbitrary")),
    )(z3, z3, z3, prev, w)


def _qkv_kernel(q_ref, k_ref, v_ref, qg_ref, kg_ref, kf_ref, vf_ref, qb_ref, kb_ref, vb_ref):
    for g in range(D_B // DH):
        sl = slice(g * DH, (g + 1) * DH)
        q = q_ref[:, sl]
        qb_ref[:, sl] = (q * _inv_rms(q) * qg_ref[...]).astype(BF16)
        k = k_ref[:, sl]
        kn = k * _inv_rms(k) * kg_ref[...]
        kf_ref[:, sl] = kn
        kb_ref[:, sl] = kn.astype(BF16)
    v = v_ref[...]
    vf_ref[...] = v
    vb_ref[...] = v.astype(BF16)


def qkv_prep(z2, qg, kg, tr=256):
    m = z2.shape[0]
    tr = min(tr, m)
    row = lambda c: pl.BlockSpec((tr, D_B), lambda i: (i, c))
    vec = pl.BlockSpec((1, DH), lambda i: (0, 0))
    return pl.pallas_call(
        _qkv_kernel,
        grid=(m // tr,),
        in_specs=[row(OFF_Q // D_B), row(OFF_K // D_B), row(OFF_VB // D_B), vec, vec],
        out_specs=[row(0)] * 5,
        out_shape=[jax.ShapeDtypeStruct((m, D_B), F32), jax.ShapeDtypeStruct((m, D_B), F32),
                   jax.ShapeDtypeStruct((m, D_B), BF16), jax.ShapeDtypeStruct((m, D_B), BF16),
                   jax.ShapeDtypeStruct((m, D_B), BF16)],
        compiler_params=_params(("parallel",)),
    )(z2, z2, z2, qg.reshape(1, DH), kg.reshape(1, DH))


def _lam(lq1, lk1, lq2, lk2, lam_init):
    a = jnp.sum(lq1[...] * lk1[...], axis=-1, keepdims=True)
    b = jnp.sum(lq2[...] * lk2[...], axis=-1, keepdims=True)
    return jnp.exp(a) - jnp.exp(b) + lam_init


def _subln(o, lam_init, g_ref):
    return (o * _inv_rms(o) * g_ref[...]) * (1.0 - lam_init)


def _attn_kernel(ns2_ref, q_ref, k_ref, v_ref, lq1, lk1, lq2, lk2, g_ref, o_ref,
                 s_ref, m_ref, l_ref, acc_ref, *, tb, lam_init):
    h = pl.program_id(1)
    qi = pl.program_id(2)
    ns2 = ns2_ref[h]
    k1 = (DH ** -0.5) * LOG2E
    m_ref[...] = jnp.full(m_ref.shape, -jnp.inf, F32)
    l_ref[...] = jnp.zeros(l_ref.shape, F32)
    acc_ref[...] = jnp.zeros(acc_ref.shape, F32)
    q = q_ref[0]
    relf = (lax.broadcasted_iota(jnp.int32, (tb, tb), 0)
            - lax.broadcasted_iota(jnp.int32, (tb, tb), 1)).astype(F32)
    lane_tiles = [slice(i * LANES, (i + 1) * LANES) for i in range(tb // LANES)]

    def score_block(j, diagonal):
        k = k_ref[0, pl.ds(pl.multiple_of(j * tb, tb), tb), :]
        distf = relf + ((qi - j) * tb).astype(F32)
        bias = distf * ns2
        for c in range(2):
            s = _dot_nt(q[:, c * DH:(c + 1) * DH], k[:, c * DH:(c + 1) * DH]) * k1 + bias
            if diagonal:
                s = jnp.where(distf >= 0, s, -jnp.inf)
            s_ref[j, c] = s
            mx = m_ref[c]
            for lt in lane_tiles:
                mx = jnp.maximum(mx, s[:, lt])
            m_ref[c] = mx

    def full_body(j, carry):
        score_block(j, False)
        return carry

    lax.fori_loop(0, qi, full_body, 0)
    score_block(qi, True)
    for c in range(2):
        m_ref[c] = jnp.broadcast_to(jnp.max(m_ref[c], axis=-1, keepdims=True), (tb, LANES))

    def exp_body(j, carry):
        for c in range(2):
            m = m_ref[c]
            part = l_ref[c]
            for lt in lane_tiles:
                e = jnp.exp2(s_ref[j, c, :, lt] - m)
                s_ref[j, c, :, lt] = e
                part = part + e
            l_ref[c] = part
        return carry

    lax.fori_loop(0, qi + 1, exp_body, 0)
    lam = _lam(lq1, lk1, lq2, lk2, lam_init)
    inv0 = jnp.broadcast_to(1.0 / jnp.sum(l_ref[0], axis=-1, keepdims=True), (tb, LANES))
    inv1 = jnp.broadcast_to(lam * (1.0 / jnp.sum(l_ref[1], axis=-1, keepdims=True)), (tb, LANES))
    l_ref[0] = inv0
    l_ref[1] = inv1

    def pv_body(j, carry):
        i0 = l_ref[0]
        i1 = l_ref[1]
        a = jnp.concatenate([s_ref[j, 0, :, lt] * i0 - s_ref[j, 1, :, lt] * i1 for lt in lane_tiles], axis=-1)
        v = v_ref[0, pl.ds(pl.multiple_of(j * tb, tb), tb), :]
        acc_ref[...] += _dot(a.astype(BF16), v)
        return carry

    lax.fori_loop(0, qi + 1, pv_body, 0)
    o_ref[0] = _subln(acc_ref[...], lam_init, g_ref).astype(o_ref.dtype)


def _alibi_slopes():
    return 2.0 ** (-8.0 * jnp.arange(1, H_B + 1, dtype=F32) / H_B)


def attn_prompt(qb, kb, vb, lam_vecs, subln_g, lam_init, tb=512):
    n, t, _ = qb.shape
    tb = min(tb, t)
    vec = pl.BlockSpec((1, DH), lambda i, h, j: (0, 0))
    return pl.pallas_call(
        functools.partial(_attn_kernel, tb=tb, lam_init=lam_init),
        grid=(n, H_B, t // tb),
        in_specs=[pl.BlockSpec(memory_space=pltpu.SMEM),
                  pl.BlockSpec((1, tb, 2 * DH), lambda i, h, j: (i, j, h)),
                  pl.BlockSpec((1, t, 2 * DH), lambda i, h, j: (i, 0, h)),
                  pl.BlockSpec((1, t, 2 * DH), lambda i, h, j: (i, 0, h)),
                  vec, vec, vec, vec,
                  pl.BlockSpec((1, 2 * DH), lambda i, h, j: (0, 0))],
        out_specs=pl.BlockSpec((1, tb, 2 * DH), lambda i, h, j: (i, j, h)),
        out_shape=jax.ShapeDtypeStruct((n, t, D_B), BF16),
        scratch_shapes=[pltpu.VMEM((t // tb, 2, tb, tb), F32),
                        pltpu.VMEM((2, tb, LANES), F32), pltpu.VMEM((2, tb, LANES), F32),
                        pltpu.VMEM((tb, 2 * DH), F32)],
        compiler_params=_params(("parallel", "parallel", "arbitrary")),
    )(-_alibi_slopes() * LOG2E, qb, kb, vb, *[x.reshape(1, DH) for x in lam_vecs],
      subln_g.reshape(1, 2 * DH))


_QROWS = 8
DECODE_T = 4


def _decode_kernel(pt_ref, ns2_ref, qh_ref, *refs, n_group, n_pages, lam_init):
    k_refs = refs[:n_group]
    v_refs = refs[n_group:2 * n_group]
    (distb_ref, kn_ref, vn_ref, distn_ref, lq1, lk1, lq2, lk2, g_ref, o_ref,
     s_ref, mx_ref, m_ref, inv_ref, acc_ref) = refs[2 * n_group:]
    ph = pl.program_id(1)
    j = pl.program_id(2)
    last = pl.num_programs(2) - 1
    k1 = (DH ** -0.5) * LOG2E
    hw = 2 * DH

    def score(h, k_bf16, dist):
        s = _dot_nt(qh_ref[0, h], k_bf16)
        return s[:_QROWS] * k1 + dist * ns2_ref[h]

    @pl.when((ph == 0) & (j == 0))
    def _():
        mx_ref[...] = jnp.full(mx_ref.shape, -jnp.inf, F32)

    @pl.when(ph == 0)
    def _():
        for g in range(n_group):
            p = j * n_group + g
            dist = distb_ref[...] - (p * PAGE_SIZE).astype(F32)
            for h in range(H_B):
                s8 = score(h, k_refs[g][0, 0, :, h * hw:(h + 1) * hw].astype(BF16), dist)
                s_ref[p, h] = s8
                mx_ref[h] = jnp.maximum(mx_ref[h], s8)

        @pl.when(j == last)
        def _():
            for h in range(H_B):
                s8 = score(h, kn_ref[0, :, h * hw:(h + 1) * hw], distn_ref[...])
                s_ref[n_pages, h] = s8
                mx = jnp.max(jnp.maximum(mx_ref[h], s8), axis=-1, keepdims=True)
                m_ref[h] = jnp.broadcast_to(mx, (_QROWS, PAGE_SIZE))

            def sum_body(p, carry):
                return tuple(carry[h] + jnp.exp2(s_ref[p, h] - m_ref[h]) for h in range(H_B))

            sums = lax.fori_loop(0, n_pages + 1, sum_body,
                                 tuple(jnp.zeros((_QROWS, PAGE_SIZE), F32) for _ in range(H_B)))
            for h in range(H_B):
                l = jnp.sum(sums[h], axis=-1, keepdims=True)
                inv_ref[h] = jnp.broadcast_to(1.0 / l, (_QROWS, PAGE_SIZE))

    @pl.when((ph == 1) & (j == 0))
    def _():
        acc_ref[...] = jnp.zeros(acc_ref.shape, F32)

    @pl.when(ph == 1)
    def _():
        lam = _lam(lq1, lk1, lq2, lk2, lam_init)

        def weights(p, h):
            e = jnp.exp2(s_ref[p, h] - m_ref[h]) * inv_ref[h]
            a = e - lam * pltpu.roll(e, DECODE_T, 0)
            return jnp.concatenate([a, jnp.zeros_like(a)], axis=0).astype(BF16)

        for g in range(n_group):
            p = j * n_group + g
            for h in range(H_B):
                acc_ref[h] += _dot(weights(p, h), v_refs[g][0, 0, :, h * hw:(h + 1) * hw].astype(BF16))

        @pl.when(j == last)
        def _():
            for h in range(H_B):
                o = acc_ref[h] + _dot(weights(n_pages, h), vn_ref[0, :, h * hw:(h + 1) * hw])
                o_ref[0, :, h * hw:(h + 1) * hw] = _subln(o[:DECODE_T], lam_init, g_ref).astype(o_ref.dtype)


def attn_decode(layer, qb, kb, vb, cache_k4, cache_v4, page_table, lam_vecs, subln_g, lam_init, t):
    assert t == DECODE_T, "decode kernel lays out DECODE_T query tokens per score tile"
    nb, n_pages = page_table.shape
    n_group = math.gcd(DECODE_PAGES, n_pages)
    n_steps = n_pages // n_group
    past = n_pages * PAGE_SIZE
    hw = 2 * DH
    q5 = qb.reshape(nb, t, H_B, 2, DH).transpose(0, 2, 3, 1, 4)
    zero = jnp.zeros_like(q5[:, :, 0])
    qh = jnp.concatenate([jnp.concatenate([q5[:, :, 0], zero], axis=-1),
                          jnp.concatenate([zero, q5[:, :, 1]], axis=-1),
                          jnp.zeros((nb, H_B, _QROWS, hw), qb.dtype)], axis=2)
    pad_rows = ((0, 0), (0, PAGE_SIZE - t), (0, 0))
    kn = jnp.pad(kb.reshape(nb, t, D_B), pad_rows)
    vn = jnp.pad(vb.reshape(nb, t, D_B), pad_rows)
    r_tok = (jnp.arange(_QROWS) % t).astype(F32)[:, None]
    pos = jnp.arange(PAGE_SIZE, dtype=F32)[None, :]
    distb = past + r_tok - pos
    distn = jnp.where((pos <= r_tok) & (pos < t), r_tok - pos, jnp.inf)
    ns2 = -_alibi_slopes() * LOG2E
    im = lambda f: (lambda b, ph, j, pt: f(b, ph, j, pt))
    vec = pl.BlockSpec((1, DH), im(lambda b, ph, j, pt: (0, 0)))
    tile = pl.BlockSpec((_QROWS, PAGE_SIZE), im(lambda b, ph, j, pt: (0, 0)))
    new_page = pl.BlockSpec((1, PAGE_SIZE, D_B), im(lambda b, ph, j, pt: (b, 0, 0)))
    k_pages = [pl.BlockSpec((1, 1, PAGE_SIZE, D_B),
                            im(lambda b, ph, j, pt, g=g: (layer, pt[b, (j + (n_steps - 1 - j) * ph) * n_group + g], 0, 0)))
               for g in range(n_group)]
    v_pages = [pl.BlockSpec((1, 1, PAGE_SIZE, D_B),
                            im(lambda b, ph, j, pt, g=g: (layer, pt[b, j * ph * n_group + g], 0, 0)))
               for g in range(n_group)]
    grid_spec = pltpu.PrefetchScalarGridSpec(
        num_scalar_prefetch=1,
        grid=(nb, 2, n_steps),
        in_specs=[pl.BlockSpec(memory_space=pltpu.SMEM),
                  pl.BlockSpec((1, H_B, 2 * _QROWS, hw), im(lambda b, ph, j, pt: (b, 0, 0, 0)))]
                 + k_pages + v_pages
                 + [tile, new_page, new_page, tile, vec, vec, vec, vec,
                    pl.BlockSpec((1, hw), im(lambda b, ph, j, pt: (0, 0)))],
        out_specs=pl.BlockSpec((1, t, D_B), im(lambda b, ph, j, pt: (b, 0, 0))),
        scratch_shapes=[pltpu.VMEM((n_pages + 1, H_B, _QROWS, PAGE_SIZE), F32),
                        pltpu.VMEM((H_B, _QROWS, PAGE_SIZE), F32),
                        pltpu.VMEM((H_B, _QROWS, PAGE_SIZE), F32),
                        pltpu.VMEM((H_B, _QROWS, PAGE_SIZE), F32),
                        pltpu.VMEM((H_B, 2 * _QROWS, hw), F32)],
    )
    out = pl.pallas_call(
        functools.partial(_decode_kernel, n_group=n_group, n_pages=n_pages, lam_init=lam_init),
        grid_spec=grid_spec,
        out_shape=jax.ShapeDtypeStruct((nb, t, D_B), BF16),
        compiler_params=_params(("parallel", "arbitrary", "arbitrary")),
    )(page_table, ns2, qh, *([cache_k4] * n_group), *([cache_v4] * n_group), distb, kn, vn, distn,
      *[x.reshape(1, DH) for x in lam_vecs], subln_g.reshape(1, hw))
    return out.reshape(nb * t, D_B)


def _merge_kernel(a_ref, b_ref, c_ref, wa_ref, wb_ref, wc_ref, ga_ref, gb_ref, gc_ref, o_ref,
                  wab_ref, wbb_ref, wcb_ref):
    @pl.when(pl.program_id(1) == 0)
    def _():
        wab_ref[...] = wa_ref[0].astype(BF16)
        wbb_ref[...] = wb_ref[0].astype(BF16)
        wcb_ref[...] = wc_ref[0].astype(BF16)

    m = jax.nn.sigmoid(ga_ref[...]) * _dot(a_ref[...], wab_ref[...])
    m = m + jax.nn.sigmoid(gb_ref[...]) * _dot(b_ref[...], wbb_ref[...])
    m = m + jax.nn.sigmoid(gc_ref[...]) * _dot(c_ref[...], wcb_ref[...])
    o_ref[...] = m.astype(o_ref.dtype)


def merge_branches(a, b, c, wa, wb, wc, layer, z2, tm=512, tn=512):
    m = a.shape[0]
    d = wa.shape[2]
    tm = min(tm, m)
    act = lambda k: pl.BlockSpec((tm, k), lambda j, i: (i, 0))
    wgt = lambda k: pl.BlockSpec((1, k, tn), lambda j, i: (layer, 0, j))
    gate = lambda br: pl.BlockSpec((tm, tn), lambda j, i: (i, (OFF_GATE + br * d) // tn + j))
    return pl.pallas_call(
        _merge_kernel,
        grid=(d // tn, m // tm),
        in_specs=[act(D_A), act(D_B), act(D_C), wgt(D_A), wgt(D_B), wgt(D_C),
                  gate(0), gate(1), gate(2)],
        out_specs=pl.BlockSpec((tm, tn), lambda j, i: (i, j)),
        out_shape=jax.ShapeDtypeStruct((m, d), BF16),
        scratch_shapes=[pltpu.VMEM((D_A, tn), BF16), pltpu.VMEM((D_B, tn), BF16),
                        pltpu.VMEM((D_C, tn), BF16)],
        compiler_params=_params(("parallel", "arbitrary"), 52),
    )(a, b, c, wa, wb, wc, z2, z2, z2)


def _router_kernel(x_ref, w_ref, b_ref, o_ref):
    logit = _dot(x_ref[...].astype(BF16), w_ref[...].astype(BF16)) + b_ref[...]
    lane = lax.broadcasted_iota(jnp.int32, logit.shape, 1).astype(F32)
    big = float(LANES)
    ninf = -jnp.inf

    def first_max(x):
        mx = jnp.max(x, axis=-1, keepdims=True)
        return mx, jnp.min(jnp.where(x == mx, lane, big), axis=-1, keepdims=True)

    gl = jnp.where(lane < N_GROUPS, logit, ninf)
    gmax, gidx = first_max(gl)
    g_top = 1.0 / jnp.sum(jnp.exp(gl - gmax), axis=-1, keepdims=True)
    lo = N_GROUPS + N_EXP_PER_GROUP * gidx
    el = jnp.where((lane >= lo) & (lane < lo + N_EXP_PER_GROUP), logit, ninf)
    l1, i1 = first_max(el)
    l2, i2 = first_max(jnp.where(lane == i1, ninf, el))
    e2 = jnp.exp(l2 - l1)
    den = 1.0 + e2
    w1 = g_top / den
    w2 = g_top * e2 / den
    out = jnp.where(lane == 0, i1 - N_GROUPS, 0.0)
    out = jnp.where(lane == 1, i2 - N_GROUPS, out)
    out = jnp.where(lane == 2, w1, out)
    out = jnp.where(lane == 3, w2, out)
    o_ref[...] = out


def router(xt, wg, bg, we, be, tr=256):
    m, d = xt.shape
    tr = min(tr, m)
    pad = LANES - N_GROUPS - N_EXPERTS
    w = jnp.concatenate([wg, we, jnp.zeros((d, pad), F32)], axis=1)
    b = jnp.concatenate([bg, be, jnp.zeros((pad,), F32)]).reshape(1, LANES)
    return pl.pallas_call(
        _router_kernel,
        grid=(m // tr,),
        in_specs=[pl.BlockSpec((tr, d), lambda i: (i, 0)),
                  pl.BlockSpec((d, LANES), lambda i: (0, 0)),
                  pl.BlockSpec((1, LANES), lambda i: (0, 0))],
        out_specs=pl.BlockSpec((tr, LANES), lambda i: (i, 0)),
        out_shape=jax.ShapeDtypeStruct((m, LANES), F32),
        compiler_params=_params(("parallel",)),
    )(xt, w, b)


_ROW_UNROLL = 8


def _moe_kernel(te_ref, nv_ref, p0_ref, tok_ref, dst_ref, x_hbm, rw_ref, wg_ref, wu_ref, wd_ref, o_hbm,
                xbuf, obuf, gsem, ssem, *, tm):
    i = pl.program_id(0)
    nt = pl.num_programs(0)
    slot = i % 2

    def start_rows(tile, start_row):
        n = nv_ref[tile]
        base = p0_ref[tile]

        def body(g, carry):
            for u in range(_ROW_UNROLL):
                start_row(base, g * _ROW_UNROLL + u)
            return carry

        def tail(r, carry):
            start_row(base, r)
            return carry

        n_full = n // _ROW_UNROLL
        lax.fori_loop(0, n_full, body, 0)
        lax.fori_loop(n_full * _ROW_UNROLL, n, tail, 0)

    def gather_rows(tile, sl):
        def start_row(base, r):
            pltpu.make_async_copy(x_hbm.at[pl.ds(tok_ref[base + r], 1)], xbuf.at[sl, pl.ds(r, 1)],
                                  gsem.at[sl]).start()
        start_rows(tile, start_row)

    def scatter_rows(tile, sl):
        def start_row(base, r):
            pltpu.make_async_copy(obuf.at[sl, pl.ds(r, 1)], o_hbm.at[pl.ds(dst_ref[base + r], 1)],
                                  ssem.at[sl]).start()
        start_rows(tile, start_row)

    def wait_rows(tile, block_copy):
        n = nv_ref[tile]
        k = tm
        while k >= 1:
            @pl.when((n & k) != 0)
            def _(k=k):
                block_copy(k).wait()
            k //= 2

    def gather_block(sl):
        return lambda k: pltpu.make_async_copy(x_hbm.at[pl.ds(0, k)], xbuf.at[sl, pl.ds(0, k)], gsem.at[sl])

    def scatter_block(sl):
        return lambda k: pltpu.make_async_copy(obuf.at[sl, pl.ds(0, k)], o_hbm.at[pl.ds(0, k)], ssem.at[sl])

    @pl.when(i == 0)
    def _():
        xbuf[...] = jnp.zeros(xbuf.shape, xbuf.dtype)
        gather_rows(0, 0)

    @pl.when(i + 1 < nt)
    def _():
        gather_rows(i + 1, 1 - slot)

    @pl.when(i >= 2)
    def _():
        wait_rows(i - 2, scatter_block(slot))

    wait_rows(i, gather_block(slot))

    @pl.when(nv_ref[i] > 0)
    def _():
        x = xbuf[slot].astype(BF16)
        hg = _dot(x, wg_ref[0, 0])
        hu = _dot(x, wu_ref[0, 0])
        hmid = (hg * jax.nn.sigmoid(hg)) * hu * rw_ref[...]
        obuf[slot] = _dot(hmid.astype(BF16), wd_ref[0, 0])

    scatter_rows(i, slot)

    @pl.when(i == nt - 1)
    def _():
        @pl.when(i >= 1)
        def _():
            wait_rows(i - 1, scatter_block(1 - slot))
        wait_rows(i, scatter_block(slot))


def moe_experts(xt, ids, wts, wg, wu, wd, layer, tm=MOE_TM):
    m, d = xt.shape
    n_pairs = 2 * m
    nt = -(-n_pairs // tm) + N_EXPERTS
    flat_e = ids.reshape(-1)
    order = jnp.argsort(flat_e, stable=True).astype(jnp.int32)
    sorted_tok = order // 2
    sorted_dst = (order % 2) * m + sorted_tok
    counts = jnp.sum(flat_e[:, None] == jnp.arange(N_EXPERTS, dtype=jnp.int32)[None, :],
                     axis=0, dtype=jnp.int32)
    tiles_e = (counts + tm - 1) // tm
    tile_end = jnp.cumsum(tiles_e)
    tile_start = tile_end - tiles_e
    pair_start = jnp.cumsum(counts) - counts
    tile_idx = jnp.arange(nt, dtype=jnp.int32)
    tile_e = jnp.minimum(jnp.searchsorted(tile_end, tile_idx, side="right"), N_EXPERTS - 1).astype(jnp.int32)
    in_expert = (tile_idx - tile_start[tile_e]) * tm
    n_valid = jnp.clip(counts[tile_e] - in_expert, 0, tm)
    n_valid = jnp.where(tile_idx < tile_end[-1], n_valid, 0).astype(jnp.int32)
    tile_p0 = jnp.where(n_valid > 0, pair_start[tile_e] + in_expert, 0).astype(jnp.int32)
    r = jnp.arange(tm, dtype=jnp.int32)[None, :]
    w_idx = jnp.minimum(tile_p0[:, None] + r, n_pairs - 1)
    row_w = jnp.where(r < n_valid[:, None], wts.reshape(-1)[order][w_idx], 0.0).reshape(nt * tm, 1)
    dff = wg.shape[-1]
    wspec = lambda shape: pl.BlockSpec((1, 1) + shape, lambda i, te, nv, p0, tk, ds: (layer, te[i], 0, 0))
    grid_spec = pltpu.PrefetchScalarGridSpec(
        num_scalar_prefetch=5,
        grid=(nt,),
        in_specs=[pl.BlockSpec(memory_space=pl.ANY),
                  pl.BlockSpec((tm, 1), lambda i, te, nv, p0, tk, ds: (i, 0)),
                  wspec((d, dff)), wspec((d, dff)), wspec((dff, d))],
        out_specs=pl.BlockSpec(memory_space=pl.ANY),
        scratch_shapes=[pltpu.VMEM((2, tm, d), F32), pltpu.VMEM((2, tm, d), F32),
                        pltpu.SemaphoreType.DMA((2,)), pltpu.SemaphoreType.DMA((2,))],
    )
    out = pl.pallas_call(
        functools.partial(_moe_kernel, tm=tm),
        grid_spec=grid_spec,
        out_shape=jax.ShapeDtypeStruct((2 * m, d), F32),
        compiler_params=_params(("arbitrary",), 56),
    )(tile_e, n_valid, tile_p0, sorted_tok, sorted_dst, xt, row_w, wg, wu, wd)
    return out.reshape(2, m, d)


def _combine_kernel(h_ref, o0_ref, o1_ref, y_ref):
    y_ref[...] = h_ref[...] + (o0_ref[0] + o1_ref[0])


def moe_combine(h, o2, row0, tr=256):
    m, d = h.shape
    tr = min(tr, m)
    return pl.pallas_call(
        _combine_kernel,
        grid=(m // tr,),
        in_specs=[pl.BlockSpec((tr, d), lambda i: (i, 0)),
                  pl.BlockSpec((1, tr, d), lambda i: (0, row0 // tr + i, 0)),
                  pl.BlockSpec((1, tr, d), lambda i: (1, row0 // tr + i, 0))],
        out_specs=pl.BlockSpec((tr, d), lambda i: (i, 0)),
        out_shape=jax.ShapeDtypeStruct((m, d), F32),
        compiler_params=_params(("parallel",)),
    )(h, o2, o2)


def kernel(x_prompt, x_sample, cache_k, cache_v, state_conv, page_table, norm1_g, w_in, sgu_norm_g, sgu_w, sgu_b, q_norm_g, k_norm_g, lam_q1, lam_k1, lam_q2, lam_k2, subln_g, conv_w, w_br_a, w_br_b, w_br_c, w_out, norm2_g, router_group_w, router_group_b, router_expert_w, router_expert_b, expert_w_gate, expert_w_up, expert_w_down):
    depth = w_in.shape[0]
    n_p, t_p, d = x_prompt.shape
    n_s, t_s, _ = x_sample.shape
    m_p, m_s = n_p * t_p, n_s * t_s
    n_in = w_in.shape[-1]
    pool = cache_k.shape[1]
    cache_k4 = cache_k.reshape(depth, pool, PAGE_SIZE, D_B)
    cache_v4 = cache_v.reshape(depth, pool, PAGE_SIZE, D_B)
    wg_bf, wu_bf, wd_bf = (w.astype(BF16) for w in (expert_w_gate, expert_w_up, expert_w_down))

    xp = x_prompt.reshape(m_p, d)
    xs = x_sample.reshape(m_s, d)
    outs = {k: [] for k in ("kp", "vp", "ks", "vs", "cp", "cs", "ap", "as")}
    for l in range(depth):
        lam_init = 0.8 - 0.6 * math.exp(-0.3 * l)
        lam_vecs = (lam_q1[l], lam_k1[l], lam_q2[l], lam_k2[l])

        def trunk(x2, mixers, tm):
            xn = rmsnorm_rows(x2, norm1_g[l], BF16)
            z2 = matmul_wcast(xn, w_in, l, F32, tm=tm, tn=512)
            a_out, b_out, c_out, extras = mixers(z2)
            mm = merge_branches(a_out, b_out, c_out, w_br_a, w_br_b, w_br_c, l, z2)
            hh = matmul_wcast(mm, w_out, l, F32, tm=tm, tn=512, residual=x2)
            xt = rmsnorm_rows(hh, norm2_g[l], F32)
            route = router(xt, router_group_w[l], router_group_b[l],
                           router_expert_w[l], router_expert_b[l])
            return hh, xt, route, extras

        def prompt_mixers(z2):
            z3 = z2.reshape(n_p, t_p, n_in)
            a_out, a_rows = sgu_prompt(z3, sgu_norm_g[l], sgu_w[l], sgu_b[l])
            c_out, c_state = short_conv(z3, jnp.zeros((n_p, CONV_W - 1, D_C), F32), conv_w[l])
            kf, vf, qb, kb, vb = qkv_prep(z2, q_norm_g[l], k_norm_g[l])
            b_out = attn_prompt(qb.reshape(n_p, t_p, D_B), kb.reshape(n_p, t_p, D_B),
                                vb.reshape(n_p, t_p, D_B), lam_vecs, subln_g[l], lam_init)
            return (a_out.reshape(m_p, D_A), b_out.reshape(m_p, D_B), c_out.reshape(m_p, D_C),
                    (kf.reshape(n_p, t_p, H_B, 2 * DH), vf.reshape(n_p, t_p, H_B, 2 * DH), c_state, a_rows))

        def sample_mixers(z2):
            z3 = z2.reshape(n_s, t_s, n_in)
            a_out, a_rows = sgu_sample(z3, sgu_norm_g[l], sgu_w[l], sgu_b[l])
            c_out, c_state = short_conv(z3, state_conv[l], conv_w[l])
            kf, vf, qb, kb, vb = qkv_prep(z2, q_norm_g[l], k_norm_g[l])
            b_out = attn_decode(l, qb, kb, vb, cache_k4, cache_v4, page_table, lam_vecs,
                                subln_g[l], lam_init, t_s)
            return (a_out.reshape(m_s, D_A), b_out, c_out.reshape(m_s, D_C),
                    (kf.reshape(n_s, t_s, H_B, 2 * DH), vf.reshape(n_s, t_s, H_B, 2 * DH), c_state, a_rows))

        h_p, xt_p, route_p, ex_p = trunk(xp, prompt_mixers, 1024)
        h_s, xt_s, route_s, ex_s = trunk(xs, sample_mixers, 1024)

        xt_all = jnp.concatenate([xt_p, xt_s], axis=0)
        route = jnp.concatenate([route_p[:, :4], route_s[:, :4]], axis=0)
        o2 = moe_experts(xt_all, route[:, :2].astype(jnp.int32), route[:, 2:4], wg_bf, wu_bf, wd_bf, l)
        xp = moe_combine(h_p, o2, 0)
        xs = moe_combine(h_s, o2, m_p)

        for key, val in zip(("kp", "vp", "cp", "ap"), ex_p):
            outs[key].append(val)
        for key, val in zip(("ks", "vs", "cs", "as"), ex_s):
            outs[key].append(val)

    st = lambda key: jnp.stack(outs[key])
    return (xp.reshape(n_p, t_p, d), xs.reshape(n_s, t_s, d), st("kp"), st("vp"), st("ks"), st("vs"),
            st("cp"), st("cs"), st("ap"), st("as"))
```

```python
import functools
import math

import jax
import jax.numpy as jnp
from jax import lax
from jax.experimental import pallas as pl
from jax.experimental.pallas import tpu as pltpu

F32 = jnp.float32
BF16 = jnp.bfloat16
EPS = 1e-6

CHUNK = 128
D_A = 1024
G_A = 8
H_B = 8
DH = 128
D_B = H_B * 2 * DH
D_C = 1024
CONV_W = 3
N_BRANCH = 3
N_GROUPS = 4
N_EXP_PER_GROUP = 8
N_EXPERTS = N_GROUPS * N_EXP_PER_GROUP
PAGE_SIZE = 128
LANES = 128

OFF_U, OFF_V = 0, D_A
OFF_Q = 2 * D_A
OFF_K = OFF_Q + D_B
OFF_VB = OFF_K + D_B
OFF_BG = OFF_VB + D_B
OFF_CG = OFF_BG + D_C
OFF_H = OFF_CG + D_C
OFF_GATE = OFF_H + D_C

LOG2E = math.log2(math.e)
MOE_TM = 256
DECODE_PAGES = 4
MIB = 1024 * 1024


def _params(sem, vmem_mib=48):
    return pltpu.CompilerParams(dimension_semantics=sem, vmem_limit_bytes=vmem_mib * MIB)


def _dot(a, b):
    return jnp.dot(a, b, preferred_element_type=F32)


def _dot_nt(a, b):
    return lax.dot_general(a, b, (((1,), (1,)), ((), ())), preferred_element_type=F32)


def _inv_rms(x):
    return lax.rsqrt(jnp.mean(x * x, axis=-1, keepdims=True) + EPS)


def _rmsnorm_kernel(x_ref, g_ref, o_ref):
    x = x_ref[...]
    o_ref[...] = (x * _inv_rms(x) * g_ref[...]).astype(o_ref.dtype)


def rmsnorm_rows(x, g, out_dtype, tr=256):
    m, d = x.shape
    tr = min(tr, m)
    return pl.pallas_call(
        _rmsnorm_kernel,
        grid=(m // tr,),
        in_specs=[pl.BlockSpec((tr, d), lambda i: (i, 0)),
                  pl.BlockSpec((1, d), lambda i: (0, 0))],
        out_specs=pl.BlockSpec((tr, d), lambda i: (i, 0)),
        out_shape=jax.ShapeDtypeStruct((m, d), out_dtype),
        compiler_params=_params(("parallel",)),
    )(x, g.reshape(1, d))


def _mm_kernel(a_ref, w_ref, *rest, has_res):
    if has_res:
        r_ref, o_ref, wb_ref = rest
    else:
        o_ref, wb_ref = rest

    @pl.when(pl.program_id(1) == 0)
    def _():
        wb_ref[...] = w_ref[0].astype(BF16)

    acc = _dot(a_ref[...], wb_ref[...])
    if has_res:
        acc = acc + r_ref[...]
    o_ref[...] = acc.astype(o_ref.dtype)


def matmul_wcast(a, w, layer, out_dtype, tm, tn, residual=None):
    m, k = a.shape
    n = w.shape[2]
    tm = min(tm, m)
    in_specs = [pl.BlockSpec((tm, k), lambda j, i: (i, 0)),
                pl.BlockSpec((1, k, tn), lambda j, i: (layer, 0, j))]
    args = [a, w]
    if residual is not None:
        in_specs.append(pl.BlockSpec((tm, tn), lambda j, i: (i, j)))
        args.append(residual)
    return pl.pallas_call(
        functools.partial(_mm_kernel, has_res=residual is not None),
        grid=(n // tn, m // tm),
        in_specs=in_specs,
        out_specs=pl.BlockSpec((tm, tn), lambda j, i: (i, j)),
        out_shape=jax.ShapeDtypeStruct((m, n), out_dtype),
        scratch_shapes=[pltpu.VMEM((k, tn), BF16)],
        compiler_params=_params(("parallel", "arbitrary"), 52),
    )(*args)


def _sgu_kernel(u_ref, v_ref, gn_ref, w_ref, bt_ref, o_ref, rows_ref):
    v = v_ref[0]
    vn = v * _inv_rms(v) * gn_ref[...]
    rows_ref[0] = vn
    vb = vn.astype(BF16)
    row = lax.broadcasted_iota(jnp.int32, (CHUNK, CHUNK), 0)
    col = lax.broadcasted_iota(jnp.int32, (CHUNK, CHUNK), 1)
    dg = D_A // G_A
    for g in range(G_A):
        wg = jnp.where(row >= col, w_ref[g], 0.0).astype(BF16)
        s = _dot(wg, vb[:, g * dg:(g + 1) * dg]) + bt_ref[:, g:g + 1]
        o_ref[0, :, g * dg:(g + 1) * dg] = (u_ref[0, :, g * dg:(g + 1) * dg] * s).astype(o_ref.dtype)


def sgu_prompt(z3, gn, w, b):
    n, t, _ = z3.shape
    nc = t // CHUNK
    return pl.pallas_call(
        _sgu_kernel,
        grid=(n, nc),
        in_specs=[pl.BlockSpec((1, CHUNK, D_A), lambda i, c: (i, c, OFF_U // D_A)),
                  pl.BlockSpec((1, CHUNK, D_A), lambda i, c: (i, c, OFF_V // D_A)),
                  pl.BlockSpec((1, D_A), lambda i, c: (0, 0)),
                  pl.BlockSpec((G_A, CHUNK, CHUNK), lambda i, c: (0, 0, 0)),
                  pl.BlockSpec((CHUNK, G_A), lambda i, c: (0, 0))],
        out_specs=[pl.BlockSpec((1, CHUNK, D_A), lambda i, c: (i, c, 0)),
                   pl.BlockSpec((1, CHUNK, D_A), lambda i, c: (i, 0, 0))],
        out_shape=[jax.ShapeDtypeStruct((n, t, D_A), BF16),
                   jax.ShapeDtypeStruct((n, CHUNK, D_A), F32)],
        compiler_params=_params(("parallel", "arbitrary")),
    )(z3, z3, gn.reshape(1, D_A), w, b.T)


def _sgu_small_kernel(u_ref, v_ref, gn_ref, wc_ref, bc_ref, o_ref, rows_ref, *, t):
    v = v_ref[...]
    vn = v * _inv_rms(v) * gn_ref[...]
    rows_ref[...] = vn
    vr = vn.astype(BF16).astype(F32)
    s = jnp.broadcast_to(bc_ref[...][None], vn.shape)
    for j in range(t):
        s = s + wc_ref[j].astype(BF16).astype(F32)[None] * vr[:, j:j + 1, :]
    o_ref[...] = (u_ref[...] * s).astype(o_ref.dtype)


def sgu_sample(z3, gn, w, b):
    n, t, _ = z3.shape
    tri = jnp.tril(jnp.ones((t, t), F32))
    w4 = w[:, :t, :t] * tri[None]
    wc = jnp.repeat(jnp.transpose(w4, (2, 1, 0)), D_A // G_A, axis=-1)
    bc = jnp.repeat(b[:, :t].T, D_A // G_A, axis=-1)
    return pl.pallas_call(
        functools.partial(_sgu_small_kernel, t=t),
        grid=(1,),
        in_specs=[pl.BlockSpec((n, t, D_A), lambda i: (0, 0, OFF_U // D_A)),
                  pl.BlockSpec((n, t, D_A), lambda i: (0, 0, OFF_V // D_A)),
                  pl.BlockSpec((1, D_A), lambda i: (0, 0)),
                  pl.BlockSpec((t, t, D_A), lambda i: (0, 0, 0)),
                  pl.BlockSpec((t, D_A), lambda i: (0, 0))],
        out_specs=[pl.BlockSpec((n, t, D_A), lambda i: (0, 0, 0)),
                   pl.BlockSpec((n, t, D_A), lambda i: (0, 0, 0))],
        out_shape=[jax.ShapeDtypeStruct((n, t, D_A), BF16),
                   jax.ShapeDtypeStruct((n, t, D_A), F32)],
        compiler_params=_params(("arbitrary",)),
    )(z3, z3, gn.reshape(1, D_A), wc, bc)


_CONV_PAD = 8


def _conv_kernel(bg_ref, cg_ref, h_ref, prev_ref, w_ref, o_ref, st_ref, xp_ref, *, tr):
    lo = _CONV_PAD - (CONV_W - 1)

    @pl.when(pl.program_id(1) == 0)
    def _():
        xp_ref[lo:_CONV_PAD, :] = prev_ref[0]

    xp_ref[_CONV_PAD:_CONV_PAD + tr, :] = cg_ref[0] * h_ref[0]
    y = w_ref[0:1, :] * xp_ref[lo:lo + tr, :]
    for j in range(1, CONV_W):
        y = y + w_ref[j:j + 1, :] * xp_ref[lo + j:lo + j + tr, :]
    o_ref[0] = (bg_ref[0] * y).astype(o_ref.dtype)
    last = xp_ref[tr + lo:tr + _CONV_PAD, :]
    st_ref[0] = last
    xp_ref[lo:_CONV_PAD, :] = last


def short_conv(z3, prev, w, tr=256):
    n, t, _ = z3.shape
    tr = min(tr, t)
    return pl.pallas_call(
        functools.partial(_conv_kernel, tr=tr),
        grid=(n, t // tr),
        in_specs=[pl.BlockSpec((1, tr, D_C), lambda i, j: (i, j, OFF_BG // D_C)),
                  pl.BlockSpec((1, tr, D_C), lambda i, j: (i, j, OFF_CG // D_C)),
                  pl.BlockSpec((1, tr, D_C), lambda i, j: (i, j, OFF_H // D_C)),
                  pl.BlockSpec((1, CONV_W - 1, D_C), lambda i, j: (i, 0, 0)),
                  pl.BlockSpec((CONV_W, D_C), lambda i, j: (0, 0))],
        out_specs=[pl.BlockSpec((1, tr, D_C), lambda i, j: (i, j, 0)),
                   pl.BlockSpec((1, CONV_W - 1, D_C), lambda i, j: (i, 0, 0))],
        out_shape=[jax.ShapeDtypeStruct((n, t, D_C), BF16),
                   jax.ShapeDtypeStruct((n, CONV_W - 1, D_C), F32)],
        scratch_shapes=[pltpu.VMEM((tr + _CONV_PAD, D_C), F32)],
        compiler_params=_params(("parallel", "arbitrary")),
    )(z3, z3, z3, prev, w)


def _qkv_kernel(q_ref, k_ref, v_ref, qg_ref, kg_ref, kf_ref, vf_ref, qb_ref, kb_ref, vb_ref):
    for g in range(D_B // DH):
        sl = slice(g * DH, (g + 1) * DH)
        q = q_ref[:, sl]
        qb_ref[:, sl] = (q * _inv_rms(q) * qg_ref[...]).astype(BF16)
        k = k_ref[:, sl]
        kn = k * _inv_rms(k) * kg_ref[...]
        kf_ref[:, sl] = kn
        kb_ref[:, sl] = kn.astype(BF16)
    v = v_ref[...]
    vf_ref[...] = v
    vb_ref[...] = v.astype(BF16)


def qkv_prep(z2, qg, kg, tr=256):
    m = z2.shape[0]
    tr = min(tr, m)
    row = lambda c: pl.BlockSpec((tr, D_B), lambda i: (i, c))
    vec = pl.BlockSpec((1, DH), lambda i: (0, 0))
    return pl.pallas_call(
        _qkv_kernel,
        grid=(m // tr,),
        in_specs=[row(OFF_Q // D_B), row(OFF_K // D_B), row(OFF_VB // D_B), vec, vec],
        out_specs=[row(0)] * 5,
        out_shape=[jax.ShapeDtypeStruct((m, D_B), F32), jax.ShapeDtypeStruct((m, D_B), F32),
                   jax.ShapeDtypeStruct((m, D_B), BF16), jax.ShapeDtypeStruct((m, D_B), BF16),
                   jax.ShapeDtypeStruct((m, D_B), BF16)],
        compiler_params=_params(("parallel",)),
    )(z2, z2, z2, qg.reshape(1, DH), kg.reshape(1, DH))


def _lam(lq1, lk1, lq2, lk2, lam_init):
    a = jnp.sum(lq1[...] * lk1[...], axis=-1, keepdims=True)
    b = jnp.sum(lq2[...] * lk2[...], axis=-1, keepdims=True)
    return jnp.exp(a) - jnp.exp(b) + lam_init


def _subln(o, lam_init, g_ref):
    return (o * _inv_rms(o) * g_ref[...]) * (1.0 - lam_init)


def _attn_kernel(ns2_ref, q_ref, k_ref, v_ref, lq1, lk1, lq2, lk2, g_ref, o_ref,
                 s_ref, m_ref, l_ref, acc_ref, *, tb, lam_init):
    h = pl.program_id(1)
    qi = pl.program_id(2)
    ns2 = ns2_ref[h]
    k1 = (DH ** -0.5) * LOG2E
    m_ref[...] = jnp.full(m_ref.shape, -jnp.inf, F32)
    l_ref[...] = jnp.zeros(l_ref.shape, F32)
    acc_ref[...] = jnp.zeros(acc_ref.shape, F32)
    q = q_ref[0]
    relf = (lax.broadcasted_iota(jnp.int32, (tb, tb), 0)
            - lax.broadcasted_iota(jnp.int32, (tb, tb), 1)).astype(F32)
    lane_tiles = [slice(i * LANES, (i + 1) * LANES) for i in range(tb // LANES)]

    def score_block(j, diagonal):
        k = k_ref[0, pl.ds(pl.multiple_of(j * tb, tb), tb), :]
        distf = relf + ((qi - j) * tb).astype(F32)
        bias = distf * ns2
        for c in range(2):
            s = _dot_nt(q[:, c * DH:(c + 1) * DH], k[:, c * DH:(c + 1) * DH]) * k1 + bias
            if diagonal:
                s = jnp.where(distf >= 0, s, -jnp.inf)
            s_ref[j, c] = s
            mx = m_ref[c]
            for lt in lane_tiles:
                mx = jnp.maximum(mx, s[:, lt])
            m_ref[c] = mx

    def full_body(j, carry):
        score_block(j, False)
        return carry

    lax.fori_loop(0, qi, full_body, 0)
    score_block(qi, True)
    for c in range(2):
        m_ref[c] = jnp.broadcast_to(jnp.max(m_ref[c], axis=-1, keepdims=True), (tb, LANES))

    def exp_body(j, carry):
        for c in range(2):
            m = m_ref[c]
            part = l_ref[c]
            for lt in lane_tiles:
                e = jnp.exp2(s_ref[j, c, :, lt] - m)
                s_ref[j, c, :, lt] = e
                part = part + e
            l_ref[c] = part
        return carry

    lax.fori_loop(0, qi + 1, exp_body, 0)
    lam = _lam(lq1, lk1, lq2, lk2, lam_init)
    inv0 = jnp.broadcast_to(1.0 / jnp.sum(l_ref[0], axis=-1, keepdims=True), (tb, LANES))
    inv1 = jnp.broadcast_to(lam * (1.0 / jnp.sum(l_ref[1], axis=-1, keepdims=True)), (tb, LANES))
    l_ref[0] = inv0
    l_ref[1] = inv1

    def pv_body(j, carry):
        i0 = l_ref[0]
        i1 = l_ref[1]
        a = jnp.concatenate([s_ref[j, 0, :, lt] * i0 - s_ref[j, 1, :, lt] * i1 for lt in lane_tiles], axis=-1)
        v = v_ref[0, pl.ds(pl.multiple_of(j * tb, tb), tb), :]
        acc_ref[...] += _dot(a.astype(BF16), v)
        return carry

    lax.fori_loop(0, qi + 1, pv_body, 0)
    o_ref[0] = _subln(acc_ref[...], lam_init, g_ref).astype(o_ref.dtype)


def _alibi_slopes():
    return 2.0 ** (-8.0 * jnp.arange(1, H_B + 1, dtype=F32) / H_B)


def attn_prompt(qb, kb, vb, lam_vecs, subln_g, lam_init, tb=512):
    n, t, _ = qb.shape
    tb = min(tb, t)
    vec = pl.BlockSpec((1, DH), lambda i, h, j: (0, 0))
    return pl.pallas_call(
        functools.partial(_attn_kernel, tb=tb, lam_init=lam_init),
        grid=(n, H_B, t // tb),
        in_specs=[pl.BlockSpec(memory_space=pltpu.SMEM),
                  pl.BlockSpec((1, tb, 2 * DH), lambda i, h, j: (i, j, h)),
                  pl.BlockSpec((1, t, 2 * DH), lambda i, h, j: (i, 0, h)),
                  pl.BlockSpec((1, t, 2 * DH), lambda i, h, j: (i, 0, h)),
                  vec, vec, vec, vec,
                  pl.BlockSpec((1, 2 * DH), lambda i, h, j: (0, 0))],
        out_specs=pl.BlockSpec((1, tb, 2 * DH), lambda i, h, j: (i, j, h)),
        out_shape=jax.ShapeDtypeStruct((n, t, D_B), BF16),
        scratch_shapes=[pltpu.VMEM((t // tb, 2, tb, tb), F32),
                        pltpu.VMEM((2, tb, LANES), F32), pltpu.VMEM((2, tb, LANES), F32),
                        pltpu.VMEM((tb, 2 * DH), F32)],
        compiler_params=_params(("parallel", "parallel", "arbitrary")),
    )(-_alibi_slopes() * LOG2E, qb, kb, vb, *[x.reshape(1, DH) for x in lam_vecs],
      subln_g.reshape(1, 2 * DH))


_QROWS = 8
DECODE_T = 4


def _decode_kernel(pt_ref, ns2_ref, qh_ref, *refs, n_group, n_pages, lam_init):
    k_refs = refs[:2 * n_group]
    v_refs = refs[2 * n_group:4 * n_group]
    (distb_ref, kn_ref, vn_ref, distn_ref, lq1, lk1, lq2, lk2, g_ref, o_ref,
     s_ref, mx_ref, m_ref, inv_ref, acc_ref) = refs[4 * n_group:]
    ph = pl.program_id(1)
    j = pl.program_id(2)
    last = pl.num_programs(2) - 1
    k1 = (DH ** -0.5) * LOG2E
    hw = 2 * DH
    first_comp = lax.broadcasted_iota(jnp.int32, (_QROWS, PAGE_SIZE), 0) < DECODE_T

    def head_rows(ref, h):
        return ref[0, 0, pl.ds(h, PAGE_SIZE, stride=H_B), :].astype(BF16)

    def score(h, k0, k1_, dist):
        s0 = _dot_nt(qh_ref[0, h, 0], k0)
        s1 = _dot_nt(qh_ref[0, h, 1], k1_)
        s = jnp.where(first_comp, s0[:_QROWS], s1[:_QROWS])
        return s * k1 + dist * ns2_ref[h]

    @pl.when((ph == 0) & (j == 0))
    def _():
        mx_ref[...] = jnp.full(mx_ref.shape, -jnp.inf, F32)

    @pl.when(ph == 0)
    def _():
        for g in range(n_group):
            p = j * n_group + g
            dist = distb_ref[...] - (p * PAGE_SIZE).astype(F32)
            for h in range(H_B):
                s8 = score(h, head_rows(k_refs[2 * g], h), head_rows(k_refs[2 * g + 1], h), dist)
                s_ref[p, h] = s8
                mx_ref[h] = jnp.maximum(mx_ref[h], s8)

        @pl.when(j == last)
        def _():
            for h in range(H_B):
                s8 = score(h, kn_ref[0, :, h * hw:h * hw + DH], kn_ref[0, :, h * hw + DH:(h + 1) * hw],
                           distn_ref[...])
                s_ref[n_pages, h] = s8
                mx = jnp.max(jnp.maximum(mx_ref[h], s8), axis=-1, keepdims=True)
                m_ref[h] = jnp.broadcast_to(mx, (_QROWS, PAGE_SIZE))

            def sum_body(p, carry):
                return tuple(carry[h] + jnp.exp2(s_ref[p, h] - m_ref[h]) for h in range(H_B))

            sums = lax.fori_loop(0, n_pages + 1, sum_body,
                                 tuple(jnp.zeros((_QROWS, PAGE_SIZE), F32) for _ in range(H_B)))
            for h in range(H_B):
                l = jnp.sum(sums[h], axis=-1, keepdims=True)
                inv_ref[h] = jnp.broadcast_to(1.0 / l, (_QROWS, PAGE_SIZE))

    @pl.when((ph == 1) & (j == 0))
    def _():
        acc_ref[...] = jnp.zeros(acc_ref.shape, F32)

    @pl.when(ph == 1)
    def _():
        lam = _lam(lq1, lk1, lq2, lk2, lam_init)

        def weights(p, h):
            e = jnp.exp2(s_ref[p, h] - m_ref[h]) * inv_ref[h]
            a = e - lam * pltpu.roll(e, DECODE_T, 0)
            return jnp.concatenate([a, jnp.zeros_like(a)], axis=0).astype(BF16)

        for g in range(n_group):
            p = j * n_group + g
            for h in range(H_B):
                a = weights(p, h)
                for c in range(2):
                    acc_ref[h, :, c * DH:(c + 1) * DH] += _dot(a, head_rows(v_refs[2 * g + c], h))

        @pl.when(j == last)
        def _():
            for h in range(H_B):
                o = acc_ref[h] + _dot(weights(n_pages, h), vn_ref[0, :, h * hw:(h + 1) * hw])
                o_ref[0, :, h * hw:(h + 1) * hw] = _subln(o[:DECODE_T], lam_init, g_ref).astype(o_ref.dtype)


def attn_decode(layer, qb, kb, vb, cache_k4, cache_v4, page_table, lam_vecs, subln_g, lam_init, t):
    assert t == DECODE_T, "decode kernel lays out DECODE_T query tokens per score tile"
    nb, n_pages = page_table.shape
    n_group = math.gcd(DECODE_PAGES, n_pages)
    n_steps = n_pages // n_group
    past = n_pages * PAGE_SIZE
    hw = 2 * DH
    rows = PAGE_SIZE * H_B
    q5 = qb.reshape(nb, t, H_B, 2, DH).transpose(0, 2, 3, 1, 4)
    qh = jnp.stack([jnp.pad(q5[:, :, 0], ((0, 0), (0, 0), (0, 2 * _QROWS - t), (0, 0))),
                    jnp.pad(q5[:, :, 1], ((0, 0), (0, 0), (t, 2 * _QROWS - 2 * t), (0, 0)))], axis=2)
    pad_rows = ((0, 0), (0, PAGE_SIZE - t), (0, 0))
    kn = jnp.pad(kb.reshape(nb, t, D_B), pad_rows)
    vn = jnp.pad(vb.reshape(nb, t, D_B), pad_rows)
    r_tok = (jnp.arange(_QROWS) % t).astype(F32)[:, None]
    pos = jnp.arange(PAGE_SIZE, dtype=F32)[None, :]
    distb = past + r_tok - pos
    distn = jnp.where((pos <= r_tok) & (pos < t), r_tok - pos, jnp.inf)
    ns2 = -_alibi_slopes() * LOG2E
    vec = pl.BlockSpec((1, DH), lambda b, ph, j, pt: (0, 0))
    tile = pl.BlockSpec((_QROWS, PAGE_SIZE), lambda b, ph, j, pt: (0, 0))
    new_page = pl.BlockSpec((1, PAGE_SIZE, D_B), lambda b, ph, j, pt: (b, 0, 0))
    k_pages = [pl.BlockSpec((1, 1, rows, DH),
                            lambda b, ph, j, pt, g=g, c=c:
                            (layer, pt[b, (j + (n_steps - 1 - j) * ph) * n_group + g], 0, c))
               for g in range(n_group) for c in range(2)]
    v_pages = [pl.BlockSpec((1, 1, rows, DH),
                            lambda b, ph, j, pt, g=g, c=c: (layer, pt[b, j * ph * n_group + g], 0, c))
               for g in range(n_group) for c in range(2)]
    grid_spec = pltpu.PrefetchScalarGridSpec(
        num_scalar_prefetch=1,
        grid=(nb, 2, n_steps),
        in_specs=[pl.BlockSpec(memory_space=pltpu.SMEM),
                  pl.BlockSpec((1, H_B, 2, 2 * _QROWS, DH), lambda b, ph, j, pt: (b, 0, 0, 0, 0))]
                 + k_pages + v_pages
                 + [tile, new_page, new_page, tile, vec, vec, vec, vec,
                    pl.BlockSpec((1, hw), lambda b, ph, j, pt: (0, 0))],
        out_specs=pl.BlockSpec((1, t, D_B), lambda b, ph, j, pt: (b, 0, 0)),
        scratch_shapes=[pltpu.VMEM((n_pages + 1, H_B, _QROWS, PAGE_SIZE), F32),
                        pltpu.VMEM((H_B, _QROWS, PAGE_SIZE), F32),
                        pltpu.VMEM((H_B, _QROWS, PAGE_SIZE), F32),
                        pltpu.VMEM((H_B, _QROWS, PAGE_SIZE), F32),
                        pltpu.VMEM((H_B, 2 * _QROWS, hw), F32)],
    )
    out = pl.pallas_call(
        functools.partial(_decode_kernel, n_group=n_group, n_pages=n_pages, lam_init=lam_init),
        grid_spec=grid_spec,
        out_shape=jax.ShapeDtypeStruct((nb, t, D_B), BF16),
        compiler_params=_params(("parallel", "arbitrary", "arbitrary")),
    )(page_table, ns2, qh, *([cache_k4] * (2 * n_group)), *([cache_v4] * (2 * n_group)), distb, kn, vn, distn,
      *[x.reshape(1, DH) for x in lam_vecs], subln_g.reshape(1, hw))
    return out.reshape(nb * t, D_B)


def _merge_kernel(a_ref, b_ref, c_ref, wa_ref, wb_ref, wc_ref, ga_ref, gb_ref, gc_ref, o_ref,
                  wab_ref, wbb_ref, wcb_ref):
    @pl.when(pl.program_id(1) == 0)
    def _():
        wab_ref[...] = wa_ref[0].astype(BF16)
        wbb_ref[...] = wb_ref[0].astype(BF16)
        wcb_ref[...] = wc_ref[0].astype(BF16)

    m = jax.nn.sigmoid(ga_ref[...]) * _dot(a_ref[...], wab_ref[...])
    m = m + jax.nn.sigmoid(gb_ref[...]) * _dot(b_ref[...], wbb_ref[...])
    m = m + jax.nn.sigmoid(gc_ref[...]) * _dot(c_ref[...], wcb_ref[...])
    o_ref[...] = m.astype(o_ref.dtype)


def merge_branches(a, b, c, wa, wb, wc, layer, z2, tm=512, tn=512):
    m = a.shape[0]
    d = wa.shape[2]
    tm = min(tm, m)
    act = lambda k: pl.BlockSpec((tm, k), lambda j, i: (i, 0))
    wgt = lambda k: pl.BlockSpec((1, k, tn), lambda j, i: (layer, 0, j))
    gate = lambda br: pl.BlockSpec((tm, tn), lambda j, i: (i, (OFF_GATE + br * d) // tn + j))
    return pl.pallas_call(
        _merge_kernel,
        grid=(d // tn, m // tm),
        in_specs=[act(D_A), act(D_B), act(D_C), wgt(D_A), wgt(D_B), wgt(D_C),
                  gate(0), gate(1), gate(2)],
        out_specs=pl.BlockSpec((tm, tn), lambda j, i: (i, j)),
        out_shape=jax.ShapeDtypeStruct((m, d), BF16),
        scratch_shapes=[pltpu.VMEM((D_A, tn), BF16), pltpu.VMEM((D_B, tn), BF16),
                        pltpu.VMEM((D_C, tn), BF16)],
        compiler_params=_params(("parallel", "arbitrary"), 52),
    )(a, b, c, wa, wb, wc, z2, z2, z2)


def _router_kernel(x_ref, w_ref, b_ref, o_ref):
    logit = _dot(x_ref[...].astype(BF16), w_ref[...].astype(BF16)) + b_ref[...]
    lane = lax.broadcasted_iota(jnp.int32, logit.shape, 1).astype(F32)
    big = float(LANES)
    ninf = -jnp.inf

    def first_max(x):
        mx = jnp.max(x, axis=-1, keepdims=True)
        return mx, jnp.min(jnp.where(x == mx, lane, big), axis=-1, keepdims=True)

    gl = jnp.where(lane < N_GROUPS, logit, ninf)
    gmax, gidx = first_max(gl)
    g_top = 1.0 / jnp.sum(jnp.exp(gl - gmax), axis=-1, keepdims=True)
    lo = N_GROUPS + N_EXP_PER_GROUP * gidx
    el = jnp.where((lane >= lo) & (lane < lo + N_EXP_PER_GROUP), logit, ninf)
    l1, i1 = first_max(el)
    l2, i2 = first_max(jnp.where(lane == i1, ninf, el))
    e2 = jnp.exp(l2 - l1)
    den = 1.0 + e2
    w1 = g_top / den
    w2 = g_top * e2 / den
    out = jnp.where(lane == 0, i1 - N_GROUPS, 0.0)
    out = jnp.where(lane == 1, i2 - N_GROUPS, out)
    out = jnp.where(lane == 2, w1, out)
    out = jnp.where(lane == 3, w2, out)
    o_ref[...] = out


def router(xt, wg, bg, we, be, tr=256):
    m, d = xt.shape
    tr = min(tr, m)
    pad = LANES - N_GROUPS - N_EXPERTS
    w = jnp.concatenate([wg, we, jnp.zeros((d, pad), F32)], axis=1)
    b = jnp.concatenate([bg, be, jnp.zeros((pad,), F32)]).reshape(1, LANES)
    return pl.pallas_call(
        _router_kernel,
        grid=(m // tr,),
        in_specs=[pl.BlockSpec((tr, d), lambda i: (i, 0)),
                  pl.BlockSpec((d, LANES), lambda i: (0, 0)),
                  pl.BlockSpec((1, LANES), lambda i: (0, 0))],
        out_specs=pl.BlockSpec((tr, LANES), lambda i: (i, 0)),
        out_shape=jax.ShapeDtypeStruct((m, LANES), F32),
        compiler_params=_params(("parallel",)),
    )(xt, w, b)


_ROW_UNROLL = 8


def _moe_kernel(te_ref, nv_ref, p0_ref, tok_ref, dst_ref, x_hbm, rw_ref, wg_ref, wu_ref, wd_ref, o_hbm,
                xbuf, obuf, gsem, ssem, *, tm):
    i = pl.program_id(0)
    nt = pl.num_programs(0)
    slot = i % 2

    def start_rows(tile, start_row):
        n = nv_ref[tile]
        base = p0_ref[tile]

        def body(g, carry):
            for u in range(_ROW_UNROLL):
                start_row(base, g * _ROW_UNROLL + u)
            return carry

        def tail(r, carry):
            start_row(base, r)
            return carry

        n_full = n // _ROW_UNROLL
        lax.fori_loop(0, n_full, body, 0)
        lax.fori_loop(n_full * _ROW_UNROLL, n, tail, 0)

    def gather_rows(tile, sl):
        def start_row(base, r):
            pltpu.make_async_copy(x_hbm.at[pl.ds(tok_ref[base + r], 1)], xbuf.at[sl, pl.ds(r, 1)],
                                  gsem.at[sl]).start()
        start_rows(tile, start_row)

    def scatter_rows(tile, sl):
        def start_row(base, r):
            pltpu.make_async_copy(obuf.at[sl, pl.ds(r, 1)], o_hbm.at[pl.ds(dst_ref[base + r], 1)],
                                  ssem.at[sl]).start()
        start_rows(tile, start_row)

    def wait_rows(tile, block_copy):
        n = nv_ref[tile]
        k = tm
        while k >= 1:
            @pl.when((n & k) != 0)
            def _(k=k):
                block_copy(k).wait()
            k //= 2

    def gather_block(sl):
        return lambda k: pltpu.make_async_copy(x_hbm.at[pl.ds(0, k)], xbuf.at[sl, pl.ds(0, k)], gsem.at[sl])

    def scatter_block(sl):
        return lambda k: pltpu.make_async_copy(obuf.at[sl, pl.ds(0, k)], o_hbm.at[pl.ds(0, k)], ssem.at[sl])

    @pl.when(i == 0)
    def _():
        xbuf[...] = jnp.zeros(xbuf.shape, xbuf.dtype)
        gather_rows(0, 0)

    @pl.when(i + 1 < nt)
    def _():
        gather_rows(i + 1, 1 - slot)

    @pl.when(i >= 2)
    def _():
        wait_rows(i - 2, scatter_block(slot))

    wait_rows(i, gather_block(slot))

    @pl.when(nv_ref[i] > 0)
    def _():
        x = xbuf[slot].astype(BF16)
        hg = _dot(x, wg_ref[0, 0])
        hu = _dot(x, wu_ref[0, 0])
        hmid = (hg * jax.nn.sigmoid(hg)) * hu * rw_ref[...]
        obuf[slot] = _dot(hmid.astype(BF16), wd_ref[0, 0])

    scatter_rows(i, slot)

    @pl.when(i == nt - 1)
    def _():
        @pl.when(i >= 1)
        def _():
            wait_rows(i - 1, scatter_block(1 - slot))
        wait_rows(i, scatter_block(slot))


def moe_experts(xt, ids, wts, wg, wu, wd, layer, tm=MOE_TM):
    m, d = xt.shape
    n_pairs = 2 * m
    nt = -(-n_pairs // tm) + N_EXPERTS
    flat_e = ids.reshape(-1)
    order = jnp.argsort(flat_e, stable=True).astype(jnp.int32)
    sorted_tok = order // 2
    sorted_dst = (order % 2) * m + sorted_tok
    counts = jnp.sum(flat_e[:, None] == jnp.arange(N_EXPERTS, dtype=jnp.int32)[None, :],
                     axis=0, dtype=jnp.int32)
    tiles_e = (counts + tm - 1) // tm
    tile_end = jnp.cumsum(tiles_e)
    tile_start = tile_end - tiles_e
    pair_start = jnp.cumsum(counts) - counts
    tile_idx = jnp.arange(nt, dtype=jnp.int32)
    tile_e = jnp.minimum(jnp.searchsorted(tile_end, tile_idx, side="right"), N_EXPERTS - 1).astype(jnp.int32)
    in_expert = (tile_idx - tile_start[tile_e]) * tm
    n_valid = jnp.clip(counts[tile_e] - in_expert, 0, tm)
    n_valid = jnp.where(tile_idx < tile_end[-1], n_valid, 0).astype(jnp.int32)
    tile_p0 = jnp.where(n_valid > 0, pair_start[tile_e] + in_expert, 0).astype(jnp.int32)
    r = jnp.arange(tm, dtype=jnp.int32)[None, :]
    w_idx = jnp.minimum(tile_p0[:, None] + r, n_pairs - 1)
    row_w = jnp.where(r < n_valid[:, None], wts.reshape(-1)[order][w_idx], 0.0).reshape(nt * tm, 1)
    dff = wg.shape[-1]
    wspec = lambda shape: pl.BlockSpec((1, 1) + shape, lambda i, te, nv, p0, tk, ds: (layer, te[i], 0, 0))
    grid_spec = pltpu.PrefetchScalarGridSpec(
        num_scalar_prefetch=5,
        grid=(nt,),
        in_specs=[pl.BlockSpec(memory_space=pl.ANY),
                  pl.BlockSpec((tm, 1), lambda i, te, nv, p0, tk, ds: (i, 0)),
                  wspec((d, dff)), wspec((d, dff)), wspec((dff, d))],
        out_specs=pl.BlockSpec(memory_space=pl.ANY),
        scratch_shapes=[pltpu.VMEM((2, tm, d), F32), pltpu.VMEM((2, tm, d), F32),
                        pltpu.SemaphoreType.DMA((2,)), pltpu.SemaphoreType.DMA((2,))],
    )
    out = pl.pallas_call(
        functools.partial(_moe_kernel, tm=tm),
        grid_spec=grid_spec,
        out_shape=jax.ShapeDtypeStruct((2 * m, d), F32),
        compiler_params=_params(("arbitrary",), 56),
    )(tile_e, n_valid, tile_p0, sorted_tok, sorted_dst, xt, row_w, wg, wu, wd)
    return out.reshape(2, m, d)


def _combine_kernel(h_ref, o0_ref, o1_ref, y_ref):
    y_ref[...] = h_ref[...] + (o0_ref[0] + o1_ref[0])


def moe_combine(h, o2, row0, tr=256):
    m, d = h.shape
    tr = min(tr, m)
    return pl.pallas_call(
        _combine_kernel,
        grid=(m // tr,),
        in_specs=[pl.BlockSpec((tr, d), lambda i: (i, 0)),
                  pl.BlockSpec((1, tr, d), lambda i: (0, row0 // tr + i, 0)),
                  pl.BlockSpec((1, tr, d), lambda i: (1, row0 // tr + i, 0))],
        out_specs=pl.BlockSpec((tr, d), lambda i: (i, 0)),
        out_shape=jax.ShapeDtypeStruct((m, d), F32),
        compiler_params=_params(("parallel",)),
    )(h, o2, o2)


def kernel(x_prompt, x_sample, cache_k, cache_v, state_conv, page_table, norm1_g, w_in, sgu_norm_g, sgu_w, sgu_b, q_norm_g, k_norm_g, lam_q1, lam_k1, lam_q2, lam_k2, subln_g, conv_w, w_br_a, w_br_b, w_br_c, w_out, norm2_g, router_group_w, router_group_b, router_expert_w, router_expert_b, expert_w_gate, expert_w_up, expert_w_down):
    depth = w_in.shape[0]
    n_p, t_p, d = x_prompt.shape
    n_s, t_s, _ = x_sample.shape
    m_p, m_s = n_p * t_p, n_s * t_s
    n_in = w_in.shape[-1]
    pool = cache_k.shape[1]
    cache_k4 = cache_k.reshape(depth, pool, PAGE_SIZE * H_B, 2 * DH)
    cache_v4 = cache_v.reshape(depth, pool, PAGE_SIZE * H_B, 2 * DH)
    wg_bf, wu_bf, wd_bf = (w.astype(BF16) for w in (expert_w_gate, expert_w_up, expert_w_down))

    xp = x_prompt.reshape(m_p, d)
    xs = x_sample.reshape(m_s, d)
    outs = {k: [] for k in ("kp", "vp", "ks", "vs", "cp", "cs", "ap", "as")}
    for l in range(depth):
        lam_init = 0.8 - 0.6 * math.exp(-0.3 * l)
        lam_vecs = (lam_q1[l], lam_k1[l], lam_q2[l], lam_k2[l])

        def trunk(x2, mixers, tm):
            xn = rmsnorm_rows(x2, norm1_g[l], BF16)
            z2 = matmul_wcast(xn, w_in, l, F32, tm=tm, tn=512)
            a_out, b_out, c_out, extras = mixers(z2)
            mm = merge_branches(a_out, b_out, c_out, w_br_a, w_br_b, w_br_c, l, z2)
            hh = matmul_wcast(mm, w_out, l, F32, tm=tm, tn=512, residual=x2)
            xt = rmsnorm_rows(hh, norm2_g[l], F32)
            route = router(xt, router_group_w[l], router_group_b[l],
                           router_expert_w[l], router_expert_b[l])
            return hh, xt, route, extras

        def prompt_mixers(z2):
            z3 = z2.reshape(n_p, t_p, n_in)
            a_out, a_rows = sgu_prompt(z3, sgu_norm_g[l], sgu_w[l], sgu_b[l])
            c_out, c_state = short_conv(z3, jnp.zeros((n_p, CONV_W - 1, D_C), F32), conv_w[l])
            kf, vf, qb, kb, vb = qkv_prep(z2, q_norm_g[l], k_norm_g[l])
            b_out = attn_prompt(qb.reshape(n_p, t_p, D_B), kb.reshape(n_p, t_p, D_B),
                                vb.reshape(n_p, t_p, D_B), lam_vecs, subln_g[l], lam_init)
            return (a_out.reshape(m_p, D_A), b_out.reshape(m_p, D_B), c_out.reshape(m_p, D_C),
                    (kf.reshape(n_p, t_p, H_B, 2 * DH), vf.reshape(n_p, t_p, H_B, 2 * DH), c_state, a_rows))

        def sample_mixers(z2):
            z3 = z2.reshape(n_s, t_s, n_in)
            a_out, a_rows = sgu_sample(z3, sgu_norm_g[l], sgu_w[l], sgu_b[l])
            c_out, c_state = short_conv(z3, state_conv[l], conv_w[l])
            kf, vf, qb, kb, vb = qkv_prep(z2, q_norm_g[l], k_norm_g[l])
            b_out = attn_decode(l, qb, kb, vb, cache_k4, cache_v4, page_table, lam_vecs,
                                subln_g[l], lam_init, t_s)
            return (a_out.reshape(m_s, D_A), b_out, c_out.reshape(m_s, D_C),
                    (kf.reshape(n_s, t_s, H_B, 2 * DH), vf.reshape(n_s, t_s, H_B, 2 * DH), c_state, a_rows))

        h_p, xt_p, route_p, ex_p = trunk(xp, prompt_mixers, 1024)
        h_s, xt_s, route_s, ex_s = trunk(xs, sample_mixers, 1024)

        xt_all = jnp.concatenate([xt_p, xt_s], axis=0)
        route = jnp.concatenate([route_p[:, :4], route_s[:, :4]], axis=0)
        o2 = moe_experts(xt_all, route[:, :2].astype(jnp.int32), route[:, 2:4], wg_bf, wu_bf, wd_bf, l)
        xp = moe_combine(h_p, o2, 0)
        xs = moe_combine(h_s, o2, m_p)

        for key, val in zip(("kp", "vp", "cp", "ap"), ex_p):
            outs[key].append(val)
        for key, val in zip(("ks", "vs", "cs", "as"), ex_s):
            outs[key].append(val)

    st = lambda key: jnp.stack(outs[key])
    return (xp.reshape(n_p, t_p, d), xs.reshape(n_s, t_s, d), st("kp"), st("vp"), st("ks"), st("vs"),
            st("cp"), st("cs"), st("ap"), st("as"))
```

```python
import functools
import math

import jax
import jax.numpy as jnp
from jax import lax
from jax.experimental import pallas as pl
from jax.experimental.pallas import tpu as pltpu

F32 = jnp.float32
BF16 = jnp.bfloat16
EPS = 1e-6

CHUNK = 128
D_A = 1024
G_A = 8
H_B = 8
DH = 128
D_B = H_B * 2 * DH
D_C = 1024
CONV_W = 3
N_BRANCH = 3
N_GROUPS = 4
N_EXP_PER_GROUP = 8
N_EXPERTS = N_GROUPS * N_EXP_PER_GROUP
PAGE_SIZE = 128
LANES = 128

OFF_U, OFF_V = 0, D_A
OFF_Q = 2 * D_A
OFF_K = OFF_Q + D_B
OFF_VB = OFF_K + D_B
OFF_BG = OFF_VB + D_B
OFF_CG = OFF_BG + D_C
OFF_H = OFF_CG + D_C
OFF_GATE = OFF_H + D_C

LOG2E = math.log2(math.e)
MOE_TM = 256
DECODE_PAGES = 4
MIB = 1024 * 1024


def _params(sem, vmem_mib=48):
    return pltpu.CompilerParams(dimension_semantics=sem, vmem_limit_bytes=vmem_mib * MIB)


def _dot(a, b):
    return jnp.dot(a, b, preferred_element_type=F32)


def _dot_nt(a, b):
    return lax.dot_general(a, b, (((1,), (1,)), ((), ())), preferred_element_type=F32)


def _inv_rms(x):
    return lax.rsqrt(jnp.mean(x * x, axis=-1, keepdims=True) + EPS)


def _rmsnorm_kernel(x_ref, g_ref, o_ref):
    x = x_ref[...]
    o_ref[...] = (x * _inv_rms(x) * g_ref[...]).astype(o_ref.dtype)


def _rmsnorm_tail_kernel(x_ref, g_ref, t_ref, o_ref, *, n_main):
    i = pl.program_id(0)

    @pl.when(i < n_main)
    def _():
        x = x_ref[...]
        o_ref[...] = (x * _inv_rms(x) * g_ref[...]).astype(o_ref.dtype)

    @pl.when(i == n_main)
    def _():
        o_ref[...] = jnp.zeros(o_ref.shape, o_ref.dtype)
        o_ref[0:t_ref.shape[0], :] = t_ref[...]


def rmsnorm_rows_with_tail(x, g, tail, tr=256):
    m, d = x.shape
    n_main = m // tr
    assert m % tr == 0 and tail.shape[0] <= tr and tail.dtype == x.dtype
    return pl.pallas_call(
        functools.partial(_rmsnorm_tail_kernel, n_main=n_main),
        grid=(n_main + 1,),
        in_specs=[pl.BlockSpec((tr, d), lambda i: (jnp.minimum(i, n_main - 1), 0)),
                  pl.BlockSpec((1, d), lambda i: (0, 0)),
                  pl.BlockSpec(tail.shape, lambda i: (0, 0))],
        out_specs=pl.BlockSpec((tr, d), lambda i: (i, 0)),
        out_shape=jax.ShapeDtypeStruct((m + tr, d), x.dtype),
        compiler_params=_params(("arbitrary",)),
    )(x, g.reshape(1, d), tail)


def rmsnorm_rows(x, g, out_dtype, tr=256):
    m, d = x.shape
    tr = min(tr, m)
    return pl.pallas_call(
        _rmsnorm_kernel,
        grid=(m // tr,),
        in_specs=[pl.BlockSpec((tr, d), lambda i: (i, 0)),
                  pl.BlockSpec((1, d), lambda i: (0, 0))],
        out_specs=pl.BlockSpec((tr, d), lambda i: (i, 0)),
        out_shape=jax.ShapeDtypeStruct((m, d), out_dtype),
        compiler_params=_params(("parallel",)),
    )(x, g.reshape(1, d))


def _mm_kernel(a_ref, w_ref, *rest, has_res):
    if has_res:
        r_ref, o_ref, wb_ref = rest
    else:
        o_ref, wb_ref = rest

    @pl.when(pl.program_id(1) == 0)
    def _():
        wb_ref[...] = w_ref[0].astype(BF16)

    acc = _dot(a_ref[...], wb_ref[...])
    if has_res:
        acc = acc + r_ref[...]
    o_ref[...] = acc.astype(o_ref.dtype)


def matmul_wcast(a, w, layer, out_dtype, tm, tn, residual=None):
    m, k = a.shape
    n = w.shape[2]
    tm = min(tm, m)
    in_specs = [pl.BlockSpec((tm, k), lambda j, i: (i, 0)),
                pl.BlockSpec((1, k, tn), lambda j, i: (layer, 0, j))]
    args = [a, w]
    if residual is not None:
        in_specs.append(pl.BlockSpec((tm, tn), lambda j, i: (i, j)))
        args.append(residual)
    return pl.pallas_call(
        functools.partial(_mm_kernel, has_res=residual is not None),
        grid=(n // tn, m // tm),
        in_specs=in_specs,
        out_specs=pl.BlockSpec((tm, tn), lambda j, i: (i, j)),
        out_shape=jax.ShapeDtypeStruct((m, n), out_dtype),
        scratch_shapes=[pltpu.VMEM((k, tn), BF16)],
        compiler_params=_params(("parallel", "arbitrary"), 52),
    )(*args)


def _sgu_kernel(u_ref, v_ref, gn_ref, w_ref, bt_ref, o_ref, rows_ref):
    v = v_ref[0]
    vn = v * _inv_rms(v) * gn_ref[...]
    rows_ref[0] = vn
    vb = vn.astype(BF16)
    row = lax.broadcasted_iota(jnp.int32, (CHUNK, CHUNK), 0)
    col = lax.broadcasted_iota(jnp.int32, (CHUNK, CHUNK), 1)
    dg = D_A // G_A
    for g in range(G_A):
        wg = jnp.where(row >= col, w_ref[g], 0.0).astype(BF16)
        s = _dot(wg, vb[:, g * dg:(g + 1) * dg]) + bt_ref[:, g:g + 1]
        o_ref[0, :, g * dg:(g + 1) * dg] = (u_ref[0, :, g * dg:(g + 1) * dg] * s).astype(o_ref.dtype)


def sgu_prompt(z3, gn, w, b):
    n, t, _ = z3.shape
    nc = t // CHUNK
    return pl.pallas_call(
        _sgu_kernel,
        grid=(n, nc),
        in_specs=[pl.BlockSpec((1, CHUNK, D_A), lambda i, c: (i, c, OFF_U // D_A)),
                  pl.BlockSpec((1, CHUNK, D_A), lambda i, c: (i, c, OFF_V // D_A)),
                  pl.BlockSpec((1, D_A), lambda i, c: (0, 0)),
                  pl.BlockSpec((G_A, CHUNK, CHUNK), lambda i, c: (0, 0, 0)),
                  pl.BlockSpec((CHUNK, G_A), lambda i, c: (0, 0))],
        out_specs=[pl.BlockSpec((1, CHUNK, D_A), lambda i, c: (i, c, 0)),
                   pl.BlockSpec((1, CHUNK, D_A), lambda i, c: (i, 0, 0))],
        out_shape=[jax.ShapeDtypeStruct((n, t, D_A), BF16),
                   jax.ShapeDtypeStruct((n, CHUNK, D_A), F32)],
        compiler_params=_params(("parallel", "arbitrary")),
    )(z3, z3, gn.reshape(1, D_A), w, b.T)


def _sgu_small_kernel(u_ref, v_ref, gn_ref, wc_ref, bc_ref, o_ref, rows_ref, *, t):
    v = v_ref[...]
    vn = v * _inv_rms(v) * gn_ref[...]
    rows_ref[...] = vn
    vr = vn.astype(BF16).astype(F32)
    s = jnp.broadcast_to(bc_ref[...][None], vn.shape)
    for j in range(t):
        s = s + wc_ref[j].astype(BF16).astype(F32)[None] * vr[:, j:j + 1, :]
    o_ref[...] = (u_ref[...] * s).astype(o_ref.dtype)


def sgu_sample(z3, gn, w, b):
    n, t, _ = z3.shape
    tri = jnp.tril(jnp.ones((t, t), F32))
    w4 = w[:, :t, :t] * tri[None]
    wc = jnp.repeat(jnp.transpose(w4, (2, 1, 0)), D_A // G_A, axis=-1)
    bc = jnp.repeat(b[:, :t].T, D_A // G_A, axis=-1)
    return pl.pallas_call(
        functools.partial(_sgu_small_kernel, t=t),
        grid=(1,),
        in_specs=[pl.BlockSpec((n, t, D_A), lambda i: (0, 0, OFF_U // D_A)),
                  pl.BlockSpec((n, t, D_A), lambda i: (0, 0, OFF_V // D_A)),
                  pl.BlockSpec((1, D_A), lambda i: (0, 0)),
                  pl.BlockSpec((t, t, D_A), lambda i: (0, 0, 0)),
                  pl.BlockSpec((t, D_A), lambda i: (0, 0))],
        out_specs=[pl.BlockSpec((n, t, D_A), lambda i: (0, 0, 0)),
                   pl.BlockSpec((n, t, D_A), lambda i: (0, 0, 0))],
        out_shape=[jax.ShapeDtypeStruct((n, t, D_A), BF16),
                   jax.ShapeDtypeStruct((n, t, D_A), F32)],
        compiler_params=_params(("arbitrary",)),
    )(z3, z3, gn.reshape(1, D_A), wc, bc)


_CONV_PAD = 8


def _conv_kernel(bg_ref, cg_ref, h_ref, prev_ref, w_ref, o_ref, st_ref, xp_ref, *, tr):
    lo = _CONV_PAD - (CONV_W - 1)

    @pl.when(pl.program_id(1) == 0)
    def _():
        xp_ref[lo:_CONV_PAD, :] = prev_ref[0]

    xp_ref[_CONV_PAD:_CONV_PAD + tr, :] = cg_ref[0] * h_ref[0]
    y = w_ref[0:1, :] * xp_ref[lo:lo + tr, :]
    for j in range(1, CONV_W):
        y = y + w_ref[j:j + 1, :] * xp_ref[lo + j:lo + j + tr, :]
    o_ref[0] = (bg_ref[0] * y).astype(o_ref.dtype)
    last = xp_ref[tr + lo:tr + _CONV_PAD, :]
    st_ref[0] = last
    xp_ref[lo:_CONV_PAD, :] = last


def short_conv(z3, prev, w, tr=256):
    n, t, _ = z3.shape
    tr = min(tr, t)
    return pl.pallas_call(
        functools.partial(_conv_kernel, tr=tr),
        grid=(n, t // tr),
        in_specs=[pl.BlockSpec((1, tr, D_C), lambda i, j: (i, j, OFF_BG // D_C)),
                  pl.BlockSpec((1, tr, D_C), lambda i, j: (i, j, OFF_CG // D_C)),
                  pl.BlockSpec((1, tr, D_C), lambda i, j: (i, j, OFF_H // D_C)),
                  pl.BlockSpec((1, CONV_W - 1, D_C), lambda i, j: (i, 0, 0)),
                  pl.BlockSpec((CONV_W, D_C), lambda i, j: (0, 0))],
        out_specs=[pl.BlockSpec((1, tr, D_C), lambda i, j: (i, j, 0)),
                   pl.BlockSpec((1, CONV_W - 1, D_C), lambda i, j: (i, 0, 0))],
        out_shape=[jax.ShapeDtypeStruct((n, t, D_C), BF16),
                   jax.ShapeDtypeStruct((n, CONV_W - 1, D_C), F32)],
        scratch_shapes=[pltpu.VMEM((tr + _CONV_PAD, D_C), F32)],
        compiler_params=_params(("parallel", "arbitrary")),
    )(z3, z3, z3, prev, w)


def _qkv_kernel(q_ref, k_ref, v_ref, qg_ref, kg_ref, kf_ref, vf_ref, qb_ref, kb_ref, vb_ref):
    for g in range(D_B // DH):
        sl = slice(g * DH, (g + 1) * DH)
        q = q_ref[:, sl]
        qb_ref[:, sl] = (q * _inv_rms(q) * qg_ref[...]).astype(BF16)
        k = k_ref[:, sl]
        kn = k * _inv_rms(k) * kg_ref[...]
        kf_ref[:, sl] = kn
        kb_ref[:, sl] = kn.astype(BF16)
    v = v_ref[...]
    vf_ref[...] = v
    vb_ref[...] = v.astype(BF16)


def qkv_prep(z2, qg, kg, tr=256):
    m = z2.shape[0]
    tr = min(tr, m)
    row = lambda c: pl.BlockSpec((tr, D_B), lambda i: (i, c))
    vec = pl.BlockSpec((1, DH), lambda i: (0, 0))
    return pl.pallas_call(
        _qkv_kernel,
        grid=(m // tr,),
        in_specs=[row(OFF_Q // D_B), row(OFF_K // D_B), row(OFF_VB // D_B), vec, vec],
        out_specs=[row(0)] * 5,
        out_shape=[jax.ShapeDtypeStruct((m, D_B), F32), jax.ShapeDtypeStruct((m, D_B), F32),
                   jax.ShapeDtypeStruct((m, D_B), BF16), jax.ShapeDtypeStruct((m, D_B), BF16),
                   jax.ShapeDtypeStruct((m, D_B), BF16)],
        compiler_params=_params(("parallel",)),
    )(z2, z2, z2, qg.reshape(1, DH), kg.reshape(1, DH))


def _lam(lq1, lk1, lq2, lk2, lam_init):
    a = jnp.sum(lq1[...] * lk1[...], axis=-1, keepdims=True)
    b = jnp.sum(lq2[...] * lk2[...], axis=-1, keepdims=True)
    return jnp.exp(a) - jnp.exp(b) + lam_init


def _subln(o, lam_init, g_ref):
    return (o * _inv_rms(o) * g_ref[...]) * (1.0 - lam_init)


def _attn_kernel(ns2_ref, q_ref, k_ref, v_ref, lq1, lk1, lq2, lk2, g_ref, o_ref,
                 s_ref, m_ref, l_ref, acc_ref, *, tb, lam_init):
    h = pl.program_id(1)
    qi = pl.program_id(2)
    ns2 = ns2_ref[h]
    k1 = (DH ** -0.5) * LOG2E
    m_ref[...] = jnp.full(m_ref.shape, -jnp.inf, F32)
    l_ref[...] = jnp.zeros(l_ref.shape, F32)
    acc_ref[...] = jnp.zeros(acc_ref.shape, F32)
    q = q_ref[0]
    relf = (lax.broadcasted_iota(jnp.int32, (tb, tb), 0)
            - lax.broadcasted_iota(jnp.int32, (tb, tb), 1)).astype(F32)
    lane_tiles = [slice(i * LANES, (i + 1) * LANES) for i in range(tb // LANES)]

    def score_block(j, diagonal):
        k = k_ref[0, pl.ds(pl.multiple_of(j * tb, tb), tb), :]
        distf = relf + ((qi - j) * tb).astype(F32)
        bias = distf * ns2
        for c in range(2):
            s = _dot_nt(q[:, c * DH:(c + 1) * DH], k[:, c * DH:(c + 1) * DH]) * k1 + bias
            if diagonal:
                s = jnp.where(distf >= 0, s, -jnp.inf)
            s_ref[j, c] = s
            mx = m_ref[c]
            for lt in lane_tiles:
                mx = jnp.maximum(mx, s[:, lt])
            m_ref[c] = mx

    def full_body(j, carry):
        score_block(j, False)
        return carry

    lax.fori_loop(0, qi, full_body, 0)
    score_block(qi, True)
    for c in range(2):
        m_ref[c] = jnp.broadcast_to(jnp.max(m_ref[c], axis=-1, keepdims=True), (tb, LANES))

    def exp_body(j, carry):
        for c in range(2):
            m = m_ref[c]
            part = l_ref[c]
            for lt in lane_tiles:
                e = jnp.exp2(s_ref[j, c, :, lt] - m)
                s_ref[j, c, :, lt] = e
                part = part + e
            l_ref[c] = part
        return carry

    lax.fori_loop(0, qi + 1, exp_body, 0)
    lam = _lam(lq1, lk1, lq2, lk2, lam_init)
    inv0 = jnp.broadcast_to(1.0 / jnp.sum(l_ref[0], axis=-1, keepdims=True), (tb, LANES))
    inv1 = jnp.broadcast_to(lam * (1.0 / jnp.sum(l_ref[1], axis=-1, keepdims=True)), (tb, LANES))
    l_ref[0] = inv0
    l_ref[1] = inv1

    def pv_body(j, carry):
        i0 = l_ref[0]
        i1 = l_ref[1]
        a = jnp.concatenate([s_ref[j, 0, :, lt] * i0 - s_ref[j, 1, :, lt] * i1 for lt in lane_tiles], axis=-1)
        v = v_ref[0, pl.ds(pl.multiple_of(j * tb, tb), tb), :]
        acc_ref[...] += _dot(a.astype(BF16), v)
        return carry

    lax.fori_loop(0, qi + 1, pv_body, 0)
    o_ref[0] = _subln(acc_ref[...], lam_init, g_ref).astype(o_ref.dtype)


def _alibi_slopes():
    return 2.0 ** (-8.0 * jnp.arange(1, H_B + 1, dtype=F32) / H_B)


def attn_prompt(qb, kb, vb, lam_vecs, subln_g, lam_init, tb=512):
    n, t, _ = qb.shape
    tb = min(tb, t)
    vec = pl.BlockSpec((1, DH), lambda i, h, j: (0, 0))
    return pl.pallas_call(
        functools.partial(_attn_kernel, tb=tb, lam_init=lam_init),
        grid=(n, H_B, t // tb),
        in_specs=[pl.BlockSpec(memory_space=pltpu.SMEM),
                  pl.BlockSpec((1, tb, 2 * DH), lambda i, h, j: (i, j, h)),
                  pl.BlockSpec((1, t, 2 * DH), lambda i, h, j: (i, 0, h)),
                  pl.BlockSpec((1, t, 2 * DH), lambda i, h, j: (i, 0, h)),
                  vec, vec, vec, vec,
                  pl.BlockSpec((1, 2 * DH), lambda i, h, j: (0, 0))],
        out_specs=pl.BlockSpec((1, tb, 2 * DH), lambda i, h, j: (i, j, h)),
        out_shape=jax.ShapeDtypeStruct((n, t, D_B), BF16),
        scratch_shapes=[pltpu.VMEM((t // tb, 2, tb, tb), F32),
                        pltpu.VMEM((2, tb, LANES), F32), pltpu.VMEM((2, tb, LANES), F32),
                        pltpu.VMEM((tb, 2 * DH), F32)],
        compiler_params=_params(("parallel", "parallel", "arbitrary")),
    )(-_alibi_slopes() * LOG2E, qb, kb, vb, *[x.reshape(1, DH) for x in lam_vecs],
      subln_g.reshape(1, 2 * DH))


_QROWS = 8
DECODE_T = 4


def _decode_kernel(pt_ref, ns2_ref, qh_ref, *refs, n_group, n_pages, lam_init):
    k_refs = refs[:2 * n_group]
    v_refs = refs[2 * n_group:4 * n_group]
    (distb_ref, kn_ref, vn_ref, distn_ref, lq1, lk1, lq2, lk2, g_ref, o_ref,
     s_ref, mx_ref, m_ref, inv_ref, acc_ref) = refs[4 * n_group:]
    ph = pl.program_id(1)
    j = pl.program_id(2)
    last = pl.num_programs(2) - 1
    k1 = (DH ** -0.5) * LOG2E
    hw = 2 * DH
    first_comp = lax.broadcasted_iota(jnp.int32, (_QROWS, PAGE_SIZE), 0) < DECODE_T

    def head_rows(ref, h):
        return ref[0, 0, pl.ds(h, PAGE_SIZE, stride=H_B), :].astype(BF16)

    def score(h, k0, k1_, dist):
        s0 = _dot_nt(qh_ref[0, h, 0], k0)
        s1 = _dot_nt(qh_ref[0, h, 1], k1_)
        s = jnp.where(first_comp, s0[:_QROWS], s1[:_QROWS])
        return s * k1 + dist * ns2_ref[h]

    @pl.when((ph == 0) & (j == 0))
    def _():
        mx_ref[...] = jnp.full(mx_ref.shape, -jnp.inf, F32)

    @pl.when(ph == 0)
    def _():
        for g in range(n_group):
            p = j * n_group + g
            dist = distb_ref[...] - (p * PAGE_SIZE).astype(F32)
            for h in range(H_B):
                s8 = score(h, head_rows(k_refs[2 * g], h), head_rows(k_refs[2 * g + 1], h), dist)
                s_ref[p, h] = s8
                mx_ref[h] = jnp.maximum(mx_ref[h], s8)

        @pl.when(j == last)
        def _():
            for h in range(H_B):
                s8 = score(h, kn_ref[0, :, h * hw:h * hw + DH], kn_ref[0, :, h * hw + DH:(h + 1) * hw],
                           distn_ref[...])
                s_ref[n_pages, h] = s8
                mx = jnp.max(jnp.maximum(mx_ref[h], s8), axis=-1, keepdims=True)
                m_ref[h] = jnp.broadcast_to(mx, (_QROWS, PAGE_SIZE))

            def sum_body(p, carry):
                return tuple(carry[h] + jnp.exp2(s_ref[p, h] - m_ref[h]) for h in range(H_B))

            sums = lax.fori_loop(0, n_pages + 1, sum_body,
                                 tuple(jnp.zeros((_QROWS, PAGE_SIZE), F32) for _ in range(H_B)))
            for h in range(H_B):
                l = jnp.sum(sums[h], axis=-1, keepdims=True)
                inv_ref[h] = jnp.broadcast_to(1.0 / l, (_QROWS, PAGE_SIZE))

    @pl.when((ph == 1) & (j == 0))
    def _():
        acc_ref[...] = jnp.zeros(acc_ref.shape, F32)

    @pl.when(ph == 1)
    def _():
        lam = _lam(lq1, lk1, lq2, lk2, lam_init)

        def weights(p, h):
            e = jnp.exp2(s_ref[p, h] - m_ref[h]) * inv_ref[h]
            a = e - lam * pltpu.roll(e, DECODE_T, 0)
            return jnp.concatenate([a, jnp.zeros_like(a)], axis=0).astype(BF16)

        for g in range(n_group):
            p = j * n_group + g
            for h in range(H_B):
                a = weights(p, h)
                for c in range(2):
                    acc_ref[h, :, c * DH:(c + 1) * DH] += _dot(a, head_rows(v_refs[2 * g + c], h))

        @pl.when(j == last)
        def _():
            for h in range(H_B):
                o = acc_ref[h] + _dot(weights(n_pages, h), vn_ref[0, :, h * hw:(h + 1) * hw])
                o_ref[0, :, h * hw:(h + 1) * hw] = _subln(o[:DECODE_T], lam_init, g_ref).astype(o_ref.dtype)


def attn_decode(layer, qb, kb, vb, cache_k4, cache_v4, page_table, lam_vecs, subln_g, lam_init, t):
    assert t == DECODE_T, "decode kernel lays out DECODE_T query tokens per score tile"
    nb, n_pages = page_table.shape
    n_group = math.gcd(DECODE_PAGES, n_pages)
    n_steps = n_pages // n_group
    past = n_pages * PAGE_SIZE
    hw = 2 * DH
    rows = PAGE_SIZE * H_B
    q5 = qb.reshape(nb, t, H_B, 2, DH).transpose(0, 2, 3, 1, 4)
    qh = jnp.stack([jnp.pad(q5[:, :, 0], ((0, 0), (0, 0), (0, 2 * _QROWS - t), (0, 0))),
                    jnp.pad(q5[:, :, 1], ((0, 0), (0, 0), (t, 2 * _QROWS - 2 * t), (0, 0)))], axis=2)
    pad_rows = ((0, 0), (0, PAGE_SIZE - t), (0, 0))
    kn = jnp.pad(kb.reshape(nb, t, D_B), pad_rows)
    vn = jnp.pad(vb.reshape(nb, t, D_B), pad_rows)
    r_tok = (jnp.arange(_QROWS) % t).astype(F32)[:, None]
    pos = jnp.arange(PAGE_SIZE, dtype=F32)[None, :]
    distb = past + r_tok - pos
    distn = jnp.where((pos <= r_tok) & (pos < t), r_tok - pos, jnp.inf)
    ns2 = -_alibi_slopes() * LOG2E
    vec = pl.BlockSpec((1, DH), lambda b, ph, j, pt: (0, 0))
    tile = pl.BlockSpec((_QROWS, PAGE_SIZE), lambda b, ph, j, pt: (0, 0))
    new_page = pl.BlockSpec((1, PAGE_SIZE, D_B), lambda b, ph, j, pt: (b, 0, 0))
    k_pages = [pl.BlockSpec((1, 1, rows, DH),
                            lambda b, ph, j, pt, g=g, c=c:
                            (layer, pt[b, (j + (n_steps - 1 - j) * ph) * n_group + g], 0, c))
               for g in range(n_group) for c in range(2)]
    v_pages = [pl.BlockSpec((1, 1, rows, DH),
                            lambda b, ph, j, pt, g=g, c=c: (layer, pt[b, j * ph * n_group + g], 0, c))
               for g in range(n_group) for c in range(2)]
    grid_spec = pltpu.PrefetchScalarGridSpec(
        num_scalar_prefetch=1,
        grid=(nb, 2, n_steps),
        in_specs=[pl.BlockSpec(memory_space=pltpu.SMEM),
                  pl.BlockSpec((1, H_B, 2, 2 * _QROWS, DH), lambda b, ph, j, pt: (b, 0, 0, 0, 0))]
                 + k_pages + v_pages
                 + [tile, new_page, new_page, tile, vec, vec, vec, vec,
                    pl.BlockSpec((1, hw), lambda b, ph, j, pt: (0, 0))],
        out_specs=pl.BlockSpec((1, t, D_B), lambda b, ph, j, pt: (b, 0, 0)),
        scratch_shapes=[pltpu.VMEM((n_pages + 1, H_B, _QROWS, PAGE_SIZE), F32),
                        pltpu.VMEM((H_B, _QROWS, PAGE_SIZE), F32),
                        pltpu.VMEM((H_B, _QROWS, PAGE_SIZE), F32),
                        pltpu.VMEM((H_B, _QROWS, PAGE_SIZE), F32),
                        pltpu.VMEM((H_B, 2 * _QROWS, hw), F32)],
    )
    out = pl.pallas_call(
        functools.partial(_decode_kernel, n_group=n_group, n_pages=n_pages, lam_init=lam_init),
        grid_spec=grid_spec,
        out_shape=jax.ShapeDtypeStruct((nb, t, D_B), BF16),
        compiler_params=_params(("parallel", "arbitrary", "arbitrary")),
    )(page_table, ns2, qh, *([cache_k4] * (2 * n_group)), *([cache_v4] * (2 * n_group)), distb, kn, vn, distn,
      *[x.reshape(1, DH) for x in lam_vecs], subln_g.reshape(1, hw))
    return out.reshape(nb * t, D_B)


def _merge_kernel(a_ref, b_ref, c_ref, wa_ref, wb_ref, wc_ref, ga_ref, gb_ref, gc_ref, o_ref,
                  wab_ref, wbb_ref, wcb_ref):
    @pl.when(pl.program_id(1) == 0)
    def _():
        wab_ref[...] = wa_ref[0].astype(BF16)
        wbb_ref[...] = wb_ref[0].astype(BF16)
        wcb_ref[...] = wc_ref[0].astype(BF16)

    m = jax.nn.sigmoid(ga_ref[...]) * _dot(a_ref[...], wab_ref[...])
    m = m + jax.nn.sigmoid(gb_ref[...]) * _dot(b_ref[...], wbb_ref[...])
    m = m + jax.nn.sigmoid(gc_ref[...]) * _dot(c_ref[...], wcb_ref[...])
    o_ref[...] = m.astype(o_ref.dtype)


def merge_branches(a, b, c, wa, wb, wc, layer, z2, tm=512, tn=512):
    m = a.shape[0]
    d = wa.shape[2]
    tm = min(tm, m)
    act = lambda k: pl.BlockSpec((tm, k), lambda j, i: (i, 0))
    wgt = lambda k: pl.BlockSpec((1, k, tn), lambda j, i: (layer, 0, j))
    gate = lambda br: pl.BlockSpec((tm, tn), lambda j, i: (i, (OFF_GATE + br * d) // tn + j))
    return pl.pallas_call(
        _merge_kernel,
        grid=(d // tn, m // tm),
        in_specs=[act(D_A), act(D_B), act(D_C), wgt(D_A), wgt(D_B), wgt(D_C),
                  gate(0), gate(1), gate(2)],
        out_specs=pl.BlockSpec((tm, tn), lambda j, i: (i, j)),
        out_shape=jax.ShapeDtypeStruct((m, d), BF16),
        scratch_shapes=[pltpu.VMEM((D_A, tn), BF16), pltpu.VMEM((D_B, tn), BF16),
                        pltpu.VMEM((D_C, tn), BF16)],
        compiler_params=_params(("parallel", "arbitrary"), 52),
    )(a, b, c, wa, wb, wc, z2, z2, z2)


def _router_kernel(x_ref, w_ref, b_ref, o_ref):
    logit = _dot(x_ref[...].astype(BF16), w_ref[...].astype(BF16)) + b_ref[...]
    lane = lax.broadcasted_iota(jnp.int32, logit.shape, 1).astype(F32)
    big = float(LANES)
    ninf = -jnp.inf

    def first_max(x):
        mx = jnp.max(x, axis=-1, keepdims=True)
        return mx, jnp.min(jnp.where(x == mx, lane, big), axis=-1, keepdims=True)

    gl = jnp.where(lane < N_GROUPS, logit, ninf)
    gmax, gidx = first_max(gl)
    g_top = 1.0 / jnp.sum(jnp.exp(gl - gmax), axis=-1, keepdims=True)
    lo = N_GROUPS + N_EXP_PER_GROUP * gidx
    el = jnp.where((lane >= lo) & (lane < lo + N_EXP_PER_GROUP), logit, ninf)
    l1, i1 = first_max(el)
    l2, i2 = first_max(jnp.where(lane == i1, ninf, el))
    e2 = jnp.exp(l2 - l1)
    den = 1.0 + e2
    w1 = g_top / den
    w2 = g_top * e2 / den
    out = jnp.where(lane == 0, i1 - N_GROUPS, 0.0)
    out = jnp.where(lane == 1, i2 - N_GROUPS, out)
    out = jnp.where(lane == 2, w1, out)
    out = jnp.where(lane == 3, w2, out)
    o_ref[...] = out


def router(xt, wg, bg, we, be, rows=None, tr=256):
    d = xt.shape[1]
    m = xt.shape[0] if rows is None else rows
    tr = min(tr, m)
    pad = LANES - N_GROUPS - N_EXPERTS
    w = jnp.concatenate([wg, we, jnp.zeros((d, pad), F32)], axis=1)
    b = jnp.concatenate([bg, be, jnp.zeros((pad,), F32)]).reshape(1, LANES)
    return pl.pallas_call(
        _router_kernel,
        grid=(m // tr,),
        in_specs=[pl.BlockSpec((tr, d), lambda i: (i, 0)),
                  pl.BlockSpec((d, LANES), lambda i: (0, 0)),
                  pl.BlockSpec((1, LANES), lambda i: (0, 0))],
        out_specs=pl.BlockSpec((tr, LANES), lambda i: (i, 0)),
        out_shape=jax.ShapeDtypeStruct((m, LANES), F32),
        compiler_params=_params(("parallel",)),
    )(xt, w, b)


_ROW_UNROLL = 8


def _moe_kernel(te_ref, nv_ref, p0_ref, tok_ref, dst_ref, x_hbm, rw_ref, wg_ref, wu_ref, wd_ref, o_hbm,
                xbuf, obuf, gsem, ssem, *, tm):
    i = pl.program_id(0)
    nt = pl.num_programs(0)
    slot = i % 2

    def start_rows(tile, start_row):
        n = nv_ref[tile]
        base = p0_ref[tile]

        def body(g, carry):
            for u in range(_ROW_UNROLL):
                start_row(base, g * _ROW_UNROLL + u)
            return carry

        def tail(r, carry):
            start_row(base, r)
            return carry

        n_full = n // _ROW_UNROLL
        lax.fori_loop(0, n_full, body, 0)
        lax.fori_loop(n_full * _ROW_UNROLL, n, tail, 0)

    def gather_rows(tile, sl):
        def start_row(base, r):
            pltpu.make_async_copy(x_hbm.at[pl.ds(tok_ref[base + r], 1)], xbuf.at[sl, pl.ds(r, 1)],
                                  gsem.at[sl]).start()
        start_rows(tile, start_row)

    def scatter_rows(tile, sl):
        def start_row(base, r):
            pltpu.make_async_copy(obuf.at[sl, pl.ds(r, 1)], o_hbm.at[pl.ds(dst_ref[base + r], 1)],
                                  ssem.at[sl]).start()
        start_rows(tile, start_row)

    def wait_rows(tile, block_copy):
        n = nv_ref[tile]
        k = tm
        while k >= 1:
            @pl.when((n & k) != 0)
            def _(k=k):
                block_copy(k).wait()
            k //= 2

    def gather_block(sl):
        return lambda k: pltpu.make_async_copy(x_hbm.at[pl.ds(0, k)], xbuf.at[sl, pl.ds(0, k)], gsem.at[sl])

    def scatter_block(sl):
        return lambda k: pltpu.make_async_copy(obuf.at[sl, pl.ds(0, k)], o_hbm.at[pl.ds(0, k)], ssem.at[sl])

    @pl.when(i == 0)
    def _():
        xbuf[...] = jnp.zeros(xbuf.shape, xbuf.dtype)
        gather_rows(0, 0)

    @pl.when(i + 1 < nt)
    def _():
        gather_rows(i + 1, 1 - slot)

    @pl.when(i >= 2)
    def _():
        wait_rows(i - 2, scatter_block(slot))

    wait_rows(i, gather_block(slot))

    @pl.when(nv_ref[i] > 0)
    def _():
        x = xbuf[slot].astype(BF16)
        hg = _dot(x, wg_ref[0, 0])
        hu = _dot(x, wu_ref[0, 0])
        hmid = (hg * jax.nn.sigmoid(hg)) * hu * rw_ref[...]
        obuf[slot] = _dot(hmid.astype(BF16), wd_ref[0, 0])

    scatter_rows(i, slot)

    @pl.when(i == nt - 1)
    def _():
        @pl.when(i >= 1)
        def _():
            wait_rows(i - 1, scatter_block(1 - slot))
        wait_rows(i, scatter_block(slot))


def moe_experts(xt, ids, wts, wg, wu, wd, layer, tm=MOE_TM):
    m, d = ids.shape[0], xt.shape[1]
    n_pairs = 2 * m
    nt = -(-n_pairs // tm) + N_EXPERTS
    flat_e = ids.reshape(-1)
    order = jnp.argsort(flat_e, stable=True).astype(jnp.int32)
    sorted_tok = order // 2
    sorted_dst = (order % 2) * m + sorted_tok
    counts = jnp.sum(flat_e[:, None] == jnp.arange(N_EXPERTS, dtype=jnp.int32)[None, :],
                     axis=0, dtype=jnp.int32)
    tiles_e = (counts + tm - 1) // tm
    tile_end = jnp.cumsum(tiles_e)
    tile_start = tile_end - tiles_e
    pair_start = jnp.cumsum(counts) - counts
    tile_idx = jnp.arange(nt, dtype=jnp.int32)
    tile_e = jnp.minimum(jnp.searchsorted(tile_end, tile_idx, side="right"), N_EXPERTS - 1).astype(jnp.int32)
    in_expert = (tile_idx - tile_start[tile_e]) * tm
    n_valid = jnp.clip(counts[tile_e] - in_expert, 0, tm)
    n_valid = jnp.where(tile_idx < tile_end[-1], n_valid, 0).astype(jnp.int32)
    tile_p0 = jnp.where(n_valid > 0, pair_start[tile_e] + in_expert, 0).astype(jnp.int32)
    r = jnp.arange(tm, dtype=jnp.int32)[None, :]
    w_idx = jnp.minimum(tile_p0[:, None] + r, n_pairs - 1)
    row_w = jnp.where(r < n_valid[:, None], wts.reshape(-1)[order][w_idx], 0.0).reshape(nt * tm, 1)
    dff = wg.shape[-1]
    wspec = lambda shape: pl.BlockSpec((1, 1) + shape, lambda i, te, nv, p0, tk, ds: (layer, te[i], 0, 0))
    grid_spec = pltpu.PrefetchScalarGridSpec(
        num_scalar_prefetch=5,
        grid=(nt,),
        in_specs=[pl.BlockSpec(memory_space=pl.ANY),
                  pl.BlockSpec((tm, 1), lambda i, te, nv, p0, tk, ds: (i, 0)),
                  wspec((d, dff)), wspec((d, dff)), wspec((dff, d))],
        out_specs=pl.BlockSpec(memory_space=pl.ANY),
        scratch_shapes=[pltpu.VMEM((2, tm, d), F32), pltpu.VMEM((2, tm, d), F32),
                        pltpu.SemaphoreType.DMA((2,)), pltpu.SemaphoreType.DMA((2,))],
    )
    out = pl.pallas_call(
        functools.partial(_moe_kernel, tm=tm),
        grid_spec=grid_spec,
        out_shape=jax.ShapeDtypeStruct((2 * m, d), F32),
        compiler_params=_params(("arbitrary",), 56),
    )(tile_e, n_valid, tile_p0, sorted_tok, sorted_dst, xt, row_w, wg, wu, wd)
    return out.reshape(2, m, d)


def _combine_kernel(h_ref, o0_ref, o1_ref, y_ref):
    y_ref[...] = h_ref[...] + (o0_ref[0] + o1_ref[0])


def moe_combine(h, o2, row0, tr=256):
    m, d = h.shape
    tr = min(tr, m)
    return pl.pallas_call(
        _combine_kernel,
        grid=(m // tr,),
        in_specs=[pl.BlockSpec((tr, d), lambda i: (i, 0)),
                  pl.BlockSpec((1, tr, d), lambda i: (0, row0 // tr + i, 0)),
                  pl.BlockSpec((1, tr, d), lambda i: (1, row0 // tr + i, 0))],
        out_specs=pl.BlockSpec((tr, d), lambda i: (i, 0)),
        out_shape=jax.ShapeDtypeStruct((m, d), F32),
        compiler_params=_params(("parallel",)),
    )(h, o2, o2)


def kernel(x_prompt, x_sample, cache_k, cache_v, state_conv, page_table, norm1_g, w_in, sgu_norm_g, sgu_w, sgu_b, q_norm_g, k_norm_g, lam_q1, lam_k1, lam_q2, lam_k2, subln_g, conv_w, w_br_a, w_br_b, w_br_c, w_out, norm2_g, router_group_w, router_group_b, router_expert_w, router_expert_b, expert_w_gate, expert_w_up, expert_w_down):
    depth = w_in.shape[0]
    n_p, t_p, d = x_prompt.shape
    n_s, t_s, _ = x_sample.shape
    m_p, m_s = n_p * t_p, n_s * t_s
    n_in = w_in.shape[-1]
    pool = cache_k.shape[1]
    cache_k4 = cache_k.reshape(depth, pool, PAGE_SIZE * H_B, 2 * DH)
    cache_v4 = cache_v.reshape(depth, pool, PAGE_SIZE * H_B, 2 * DH)
    wg_bf, wu_bf, wd_bf = (w.astype(BF16) for w in (expert_w_gate, expert_w_up, expert_w_down))

    xp = x_prompt.reshape(m_p, d)
    xs = x_sample.reshape(m_s, d)
    outs = {k: [] for k in ("kp", "vp", "ks", "vs", "cp", "cs", "ap", "as")}
    for l in range(depth):
        lam_init = 0.8 - 0.6 * math.exp(-0.3 * l)
        lam_vecs = (lam_q1[l], lam_k1[l], lam_q2[l], lam_k2[l])

        def trunk(x2, mixers, tm, xt_tail=None):
            xn = rmsnorm_rows(x2, norm1_g[l], BF16)
            z2 = matmul_wcast(xn, w_in, l, F32, tm=tm, tn=512)
            a_out, b_out, c_out, extras = mixers(z2)
            mm = merge_branches(a_out, b_out, c_out, w_br_a, w_br_b, w_br_c, l, z2)
            hh = matmul_wcast(mm, w_out, l, F32, tm=tm, tn=512, residual=x2)
            if xt_tail is None:
                xt = rmsnorm_rows(hh, norm2_g[l], F32)
            else:
                xt = rmsnorm_rows_with_tail(hh, norm2_g[l], xt_tail)
            route = router(xt, router_group_w[l], router_group_b[l],
                           router_expert_w[l], router_expert_b[l], rows=x2.shape[0])
            return hh, xt, route, extras

        def prompt_mixers(z2):
            z3 = z2.reshape(n_p, t_p, n_in)
            a_out, a_rows = sgu_prompt(z3, sgu_norm_g[l], sgu_w[l], sgu_b[l])
            c_out, c_state = short_conv(z3, jnp.zeros((n_p, CONV_W - 1, D_C), F32), conv_w[l])
            kf, vf, qb, kb, vb = qkv_prep(z2, q_norm_g[l], k_norm_g[l])
            b_out = attn_prompt(qb.reshape(n_p, t_p, D_B), kb.reshape(n_p, t_p, D_B),
                                vb.reshape(n_p, t_p, D_B), lam_vecs, subln_g[l], lam_init)
            return (a_out.reshape(m_p, D_A), b_out.reshape(m_p, D_B), c_out.reshape(m_p, D_C),
                    (kf.reshape(n_p, t_p, H_B, 2 * DH), vf.reshape(n_p, t_p, H_B, 2 * DH), c_state, a_rows))

        def sample_mixers(z2):
            z3 = z2.reshape(n_s, t_s, n_in)
            a_out, a_rows = sgu_sample(z3, sgu_norm_g[l], sgu_w[l], sgu_b[l])
            c_out, c_state = short_conv(z3, state_conv[l], conv_w[l])
            kf, vf, qb, kb, vb = qkv_prep(z2, q_norm_g[l], k_norm_g[l])
            b_out = attn_decode(l, qb, kb, vb, cache_k4, cache_v4, page_table, lam_vecs,
                                subln_g[l], lam_init, t_s)
            return (a_out.reshape(m_s, D_A), b_out, c_out.reshape(m_s, D_C),
                    (kf.reshape(n_s, t_s, H_B, 2 * DH), vf.reshape(n_s, t_s, H_B, 2 * DH), c_state, a_rows))

        h_s, xt_s, route_s, ex_s = trunk(xs, sample_mixers, 1024)
        h_p, xt_all, route_p, ex_p = trunk(xp, prompt_mixers, 1024, xt_tail=xt_s)
        route = jnp.concatenate([route_p[:, :4], route_s[:, :4]], axis=0)
        o2 = moe_experts(xt_all, route[:, :2].astype(jnp.int32), route[:, 2:4], wg_bf, wu_bf, wd_bf, l)
        xp = moe_combine(h_p, o2, 0)
        xs = moe_combine(h_s, o2, m_p)

        for key, val in zip(("kp", "vp", "cp", "ap"), ex_p):
            outs[key].append(val)
        for key, val in zip(("ks", "vs", "cs", "as"), ex_s):
            outs[key].append(val)

    st = lambda key: jnp.stack(outs[key])
    return (xp.reshape(n_p, t_p, d), xs.reshape(n_s, t_s, d), st("kp"), st("vp"), st("ks"), st("vs"),
            st("cp"), st("cs"), st("ap"), st("as"))
```

```python
import functools
import math

import jax
import jax.numpy as jnp
from jax import lax
from jax.experimental import pallas as pl
from jax.experimental.pallas import tpu as pltpu

F32 = jnp.float32
BF16 = jnp.bfloat16
EPS = 1e-6

CHUNK = 128
D_A = 1024
G_A = 8
H_B = 8
DH = 128
D_B = H_B * 2 * DH
D_C = 1024
CONV_W = 3
N_BRANCH = 3
N_GROUPS = 4
N_EXP_PER_GROUP = 8
N_EXPERTS = N_GROUPS * N_EXP_PER_GROUP
PAGE_SIZE = 128
LANES = 128

OFF_U, OFF_V = 0, D_A
OFF_Q = 2 * D_A
OFF_K = OFF_Q + D_B
OFF_VB = OFF_K + D_B
OFF_BG = OFF_VB + D_B
OFF_CG = OFF_BG + D_C
OFF_H = OFF_CG + D_C
OFF_GATE = OFF_H + D_C

LOG2E = math.log2(math.e)
MOE_TM = 256
DECODE_PAGES = 4
MIB = 1024 * 1024


def _params(sem, vmem_mib=48):
    return pltpu.CompilerParams(dimension_semantics=sem, vmem_limit_bytes=vmem_mib * MIB)


def _dot(a, b):
    return jnp.dot(a, b, preferred_element_type=F32)


def _dot_nt(a, b):
    return lax.dot_general(a, b, (((1,), (1,)), ((), ())), preferred_element_type=F32)


def _inv_rms(x):
    return lax.rsqrt(jnp.mean(x * x, axis=-1, keepdims=True) + EPS)


def _rmsnorm_kernel(x_ref, g_ref, o_ref):
    x = x_ref[...]
    o_ref[...] = (x * _inv_rms(x) * g_ref[...]).astype(o_ref.dtype)


def _rmsnorm_tail_kernel(x_ref, g_ref, t_ref, o_ref, *, n_main):
    i = pl.program_id(0)

    @pl.when(i < n_main)
    def _():
        x = x_ref[...]
        o_ref[...] = (x * _inv_rms(x) * g_ref[...]).astype(o_ref.dtype)

    @pl.when(i == n_main)
    def _():
        o_ref[...] = jnp.zeros(o_ref.shape, o_ref.dtype)
        o_ref[0:t_ref.shape[0], :] = t_ref[...]


def rmsnorm_rows_with_tail(x, g, tail, tr=256):
    m, d = x.shape
    n_main = m // tr
    assert m % tr == 0 and tail.shape[0] <= tr and tail.dtype == x.dtype
    return pl.pallas_call(
        functools.partial(_rmsnorm_tail_kernel, n_main=n_main),
        grid=(n_main + 1,),
        in_specs=[pl.BlockSpec((tr, d), lambda i: (jnp.minimum(i, n_main - 1), 0)),
                  pl.BlockSpec((1, d), lambda i: (0, 0)),
                  pl.BlockSpec(tail.shape, lambda i: (0, 0))],
        out_specs=pl.BlockSpec((tr, d), lambda i: (i, 0)),
        out_shape=jax.ShapeDtypeStruct((m + tr, d), x.dtype),
        compiler_params=_params(("arbitrary",)),
    )(x, g.reshape(1, d), tail)


def rmsnorm_rows(x, g, out_dtype, tr=256):
    m, d = x.shape
    tr = min(tr, m)
    return pl.pallas_call(
        _rmsnorm_kernel,
        grid=(m // tr,),
        in_specs=[pl.BlockSpec((tr, d), lambda i: (i, 0)),
                  pl.BlockSpec((1, d), lambda i: (0, 0))],
        out_specs=pl.BlockSpec((tr, d), lambda i: (i, 0)),
        out_shape=jax.ShapeDtypeStruct((m, d), out_dtype),
        compiler_params=_params(("parallel",)),
    )(x, g.reshape(1, d))


def _mm_kernel(a_ref, w_ref, *rest, has_res):
    if has_res:
        r_ref, o_ref, wb_ref = rest
    else:
        o_ref, wb_ref = rest

    @pl.when(pl.program_id(1) == 0)
    def _():
        wb_ref[...] = w_ref[0].astype(BF16)

    acc = _dot(a_ref[...], wb_ref[...])
    if has_res:
        acc = acc + r_ref[...]
    o_ref[...] = acc.astype(o_ref.dtype)


def matmul_wcast(a, w, layer, out_dtype, tm, tn, residual=None):
    m, k = a.shape
    n = w.shape[2]
    tm = min(tm, m)
    in_specs = [pl.BlockSpec((tm, k), lambda j, i: (i, 0)),
                pl.BlockSpec((1, k, tn), lambda j, i: (layer, 0, j))]
    args = [a, w]
    if residual is not None:
        in_specs.append(pl.BlockSpec((tm, tn), lambda j, i: (i, j)))
        args.append(residual)
    return pl.pallas_call(
        functools.partial(_mm_kernel, has_res=residual is not None),
        grid=(n // tn, m // tm),
        in_specs=in_specs,
        out_specs=pl.BlockSpec((tm, tn), lambda j, i: (i, j)),
        out_shape=jax.ShapeDtypeStruct((m, n), out_dtype),
        scratch_shapes=[pltpu.VMEM((k, tn), BF16)],
        compiler_params=_params(("parallel", "arbitrary"), 52),
    )(*args)


def _sgu_kernel(u_ref, v_ref, gn_ref, w_ref, bt_ref, o_ref, rows_ref):
    v = v_ref[0]
    vn = v * _inv_rms(v) * gn_ref[...]
    rows_ref[0] = vn
    vb = vn.astype(BF16)
    row = lax.broadcasted_iota(jnp.int32, (CHUNK, CHUNK), 0)
    col = lax.broadcasted_iota(jnp.int32, (CHUNK, CHUNK), 1)
    dg = D_A // G_A
    for g in range(G_A):
        wg = jnp.where(row >= col, w_ref[g], 0.0).astype(BF16)
        s = _dot(wg, vb[:, g * dg:(g + 1) * dg]) + bt_ref[:, g:g + 1]
        o_ref[0, :, g * dg:(g + 1) * dg] = (u_ref[0, :, g * dg:(g + 1) * dg] * s).astype(o_ref.dtype)


def sgu_prompt(z3, gn, w, b):
    n, t, _ = z3.shape
    nc = t // CHUNK
    return pl.pallas_call(
        _sgu_kernel,
        grid=(n, nc),
        in_specs=[pl.BlockSpec((1, CHUNK, D_A), lambda i, c: (i, c, OFF_U // D_A)),
                  pl.BlockSpec((1, CHUNK, D_A), lambda i, c: (i, c, OFF_V // D_A)),
                  pl.BlockSpec((1, D_A), lambda i, c: (0, 0)),
                  pl.BlockSpec((G_A, CHUNK, CHUNK), lambda i, c: (0, 0, 0)),
                  pl.BlockSpec((CHUNK, G_A), lambda i, c: (0, 0))],
        out_specs=[pl.BlockSpec((1, CHUNK, D_A), lambda i, c: (i, c, 0)),
                   pl.BlockSpec((1, CHUNK, D_A), lambda i, c: (i, 0, 0))],
        out_shape=[jax.ShapeDtypeStruct((n, t, D_A), BF16),
                   jax.ShapeDtypeStruct((n, CHUNK, D_A), F32)],
        compiler_params=_params(("parallel", "arbitrary")),
    )(z3, z3, gn.reshape(1, D_A), w, b.T)


def _sgu_small_kernel(u_ref, v_ref, gn_ref, wc_ref, bc_ref, o_ref, rows_ref, *, t):
    v = v_ref[...]
    vn = v * _inv_rms(v) * gn_ref[...]
    rows_ref[...] = vn
    vr = vn.astype(BF16).astype(F32)
    s = jnp.broadcast_to(bc_ref[...][None], vn.shape)
    for j in range(t):
        s = s + wc_ref[j].astype(BF16).astype(F32)[None] * vr[:, j:j + 1, :]
    o_ref[...] = (u_ref[...] * s).astype(o_ref.dtype)


def sgu_sample(z3, gn, w, b):
    n, t, _ = z3.shape
    tri = jnp.tril(jnp.ones((t, t), F32))
    w4 = w[:, :t, :t] * tri[None]
    wc = jnp.repeat(jnp.transpose(w4, (2, 1, 0)), D_A // G_A, axis=-1)
    bc = jnp.repeat(b[:, :t].T, D_A // G_A, axis=-1)
    return pl.pallas_call(
        functools.partial(_sgu_small_kernel, t=t),
        grid=(1,),
        in_specs=[pl.BlockSpec((n, t, D_A), lambda i: (0, 0, OFF_U // D_A)),
                  pl.BlockSpec((n, t, D_A), lambda i: (0, 0, OFF_V // D_A)),
                  pl.BlockSpec((1, D_A), lambda i: (0, 0)),
                  pl.BlockSpec((t, t, D_A), lambda i: (0, 0, 0)),
                  pl.BlockSpec((t, D_A), lambda i: (0, 0))],
        out_specs=[pl.BlockSpec((n, t, D_A), lambda i: (0, 0, 0)),
                   pl.BlockSpec((n, t, D_A), lambda i: (0, 0, 0))],
        out_shape=[jax.ShapeDtypeStruct((n, t, D_A), BF16),
                   jax.ShapeDtypeStruct((n, t, D_A), F32)],
        compiler_params=_params(("arbitrary",)),
    )(z3, z3, gn.reshape(1, D_A), wc, bc)


_CONV_PAD = 8


def _conv_kernel(bg_ref, cg_ref, h_ref, prev_ref, w_ref, o_ref, st_ref, xp_ref, *, tr):
    lo = _CONV_PAD - (CONV_W - 1)

    @pl.when(pl.program_id(1) == 0)
    def _():
        xp_ref[lo:_CONV_PAD, :] = prev_ref[0]

    xp_ref[_CONV_PAD:_CONV_PAD + tr, :] = cg_ref[0] * h_ref[0]
    y = w_ref[0:1, :] * xp_ref[lo:lo + tr, :]
    for j in range(1, CONV_W):
        y = y + w_ref[j:j + 1, :] * xp_ref[lo + j:lo + j + tr, :]
    o_ref[0] = (bg_ref[0] * y).astype(o_ref.dtype)
    last = xp_ref[tr + lo:tr + _CONV_PAD, :]
    st_ref[0] = last
    xp_ref[lo:_CONV_PAD, :] = last


def short_conv(z3, prev, w, tr=256):
    n, t, _ = z3.shape
    tr = min(tr, t)
    return pl.pallas_call(
        functools.partial(_conv_kernel, tr=tr),
        grid=(n, t // tr),
        in_specs=[pl.BlockSpec((1, tr, D_C), lambda i, j: (i, j, OFF_BG // D_C)),
                  pl.BlockSpec((1, tr, D_C), lambda i, j: (i, j, OFF_CG // D_C)),
                  pl.BlockSpec((1, tr, D_C), lambda i, j: (i, j, OFF_H // D_C)),
                  pl.BlockSpec((1, CONV_W - 1, D_C), lambda i, j: (i, 0, 0)),
                  pl.BlockSpec((CONV_W, D_C), lambda i, j: (0, 0))],
        out_specs=[pl.BlockSpec((1, tr, D_C), lambda i, j: (i, j, 0)),
                   pl.BlockSpec((1, CONV_W - 1, D_C), lambda i, j: (i, 0, 0))],
        out_shape=[jax.ShapeDtypeStruct((n, t, D_C), BF16),
                   jax.ShapeDtypeStruct((n, CONV_W - 1, D_C), F32)],
        scratch_shapes=[pltpu.VMEM((tr + _CONV_PAD, D_C), F32)],
        compiler_params=_params(("parallel", "arbitrary")),
    )(z3, z3, z3, prev, w)


def _qkv_kernel(q_ref, k_ref, v_ref, qg_ref, kg_ref, kf_ref, vf_ref, qb_ref, kb_ref, vb_ref):
    for g in range(D_B // DH):
        sl = slice(g * DH, (g + 1) * DH)
        q = q_ref[:, sl]
        qb_ref[:, sl] = (q * _inv_rms(q) * qg_ref[...]).astype(BF16)
        k = k_ref[:, sl]
        kn = k * _inv_rms(k) * kg_ref[...]
        kf_ref[:, sl] = kn
        kb_ref[:, sl] = kn.astype(BF16)
    v = v_ref[...]
    vf_ref[...] = v
    vb_ref[...] = v.astype(BF16)


def qkv_prep(z2, qg, kg, tr=256):
    m = z2.shape[0]
    tr = min(tr, m)
    row = lambda c: pl.BlockSpec((tr, D_B), lambda i: (i, c))
    vec = pl.BlockSpec((1, DH), lambda i: (0, 0))
    return pl.pallas_call(
        _qkv_kernel,
        grid=(m // tr,),
        in_specs=[row(OFF_Q // D_B), row(OFF_K // D_B), row(OFF_VB // D_B), vec, vec],
        out_specs=[row(0)] * 5,
        out_shape=[jax.ShapeDtypeStruct((m, D_B), F32), jax.ShapeDtypeStruct((m, D_B), F32),
                   jax.ShapeDtypeStruct((m, D_B), BF16), jax.ShapeDtypeStruct((m, D_B), BF16),
                   jax.ShapeDtypeStruct((m, D_B), BF16)],
        compiler_params=_params(("parallel",)),
    )(z2, z2, z2, qg.reshape(1, DH), kg.reshape(1, DH))


def _lam(lq1, lk1, lq2, lk2, lam_init):
    a = jnp.sum(lq1[...] * lk1[...], axis=-1, keepdims=True)
    b = jnp.sum(lq2[...] * lk2[...], axis=-1, keepdims=True)
    return jnp.exp(a) - jnp.exp(b) + lam_init


def _subln(o, lam_init, g_ref):
    return (o * _inv_rms(o) * g_ref[...]) * (1.0 - lam_init)


def _attn_kernel(ns2_ref, q_ref, k_ref, v_ref, lq1, lk1, lq2, lk2, g_ref, o_ref,
                 s_ref, m_ref, l_ref, acc_ref, *, tb, lam_init):
    h = pl.program_id(1)
    qi = pl.program_id(2)
    ns2 = ns2_ref[h]
    k1 = (DH ** -0.5) * LOG2E
    m_ref[...] = jnp.full(m_ref.shape, -jnp.inf, F32)
    l_ref[...] = jnp.zeros(l_ref.shape, F32)
    acc_ref[...] = jnp.zeros(acc_ref.shape, F32)
    q = q_ref[0]
    relf = (lax.broadcasted_iota(jnp.int32, (tb, tb), 0)
            - lax.broadcasted_iota(jnp.int32, (tb, tb), 1)).astype(F32)
    lane_tiles = [slice(i * LANES, (i + 1) * LANES) for i in range(tb // LANES)]

    def score_block(j, diagonal):
        k = k_ref[0, pl.ds(pl.multiple_of(j * tb, tb), tb), :]
        distf = relf + ((qi - j) * tb).astype(F32)
        bias = distf * ns2
        for c in range(2):
            s = _dot_nt(q[:, c * DH:(c + 1) * DH], k[:, c * DH:(c + 1) * DH]) * k1 + bias
            if diagonal:
                s = jnp.where(distf >= 0, s, -jnp.inf)
            s_ref[j, c] = s
            mx = m_ref[c]
            for lt in lane_tiles:
                mx = jnp.maximum(mx, s[:, lt])
            m_ref[c] = mx

    def full_body(j, carry):
        score_block(j, False)
        return carry

    lax.fori_loop(0, qi, full_body, 0)
    score_block(qi, True)
    for c in range(2):
        m_ref[c] = jnp.broadcast_to(jnp.max(m_ref[c], axis=-1, keepdims=True), (tb, LANES))

    def exp_body(j, carry):
        for c in range(2):
            m = m_ref[c]
            part = l_ref[c]
            for lt in lane_tiles:
                e = jnp.exp2(s_ref[j, c, :, lt] - m)
                s_ref[j, c, :, lt] = e
                part = part + e
            l_ref[c] = part
        return carry

    lax.fori_loop(0, qi + 1, exp_body, 0)
    lam = _lam(lq1, lk1, lq2, lk2, lam_init)
    inv0 = jnp.broadcast_to(1.0 / jnp.sum(l_ref[0], axis=-1, keepdims=True), (tb, LANES))
    inv1 = jnp.broadcast_to(lam * (1.0 / jnp.sum(l_ref[1], axis=-1, keepdims=True)), (tb, LANES))
    l_ref[0] = inv0
    l_ref[1] = inv1

    def pv_body(j, carry):
        i0 = l_ref[0]
        i1 = l_ref[1]
        a = jnp.concatenate([s_ref[j, 0, :, lt] * i0 - s_ref[j, 1, :, lt] * i1 for lt in lane_tiles], axis=-1)
        v = v_ref[0, pl.ds(pl.multiple_of(j * tb, tb), tb), :]
        acc_ref[...] += _dot(a.astype(BF16), v)
        return carry

    lax.fori_loop(0, qi + 1, pv_body, 0)
    o_ref[0] = _subln(acc_ref[...], lam_init, g_ref).astype(o_ref.dtype)


def _alibi_slopes():
    return 2.0 ** (-8.0 * jnp.arange(1, H_B + 1, dtype=F32) / H_B)


def attn_prompt(qb, kb, vb, lam_vecs, subln_g, lam_init, tb=512):
    n, t, _ = qb.shape
    tb = min(tb, t)
    vec = pl.BlockSpec((1, DH), lambda i, h, j: (0, 0))
    return pl.pallas_call(
        functools.partial(_attn_kernel, tb=tb, lam_init=lam_init),
        grid=(n, H_B, t // tb),
        in_specs=[pl.BlockSpec(memory_space=pltpu.SMEM),
                  pl.BlockSpec((1, tb, 2 * DH), lambda i, h, j: (i, j, h)),
                  pl.BlockSpec((1, t, 2 * DH), lambda i, h, j: (i, 0, h)),
                  pl.BlockSpec((1, t, 2 * DH), lambda i, h, j: (i, 0, h)),
                  vec, vec, vec, vec,
                  pl.BlockSpec((1, 2 * DH), lambda i, h, j: (0, 0))],
        out_specs=pl.BlockSpec((1, tb, 2 * DH), lambda i, h, j: (i, j, h)),
        out_shape=jax.ShapeDtypeStruct((n, t, D_B), BF16),
        scratch_shapes=[pltpu.VMEM((t // tb, 2, tb, tb), F32),
                        pltpu.VMEM((2, tb, LANES), F32), pltpu.VMEM((2, tb, LANES), F32),
                        pltpu.VMEM((tb, 2 * DH), F32)],
        compiler_params=_params(("parallel", "parallel", "arbitrary")),
    )(-_alibi_slopes() * LOG2E, qb, kb, vb, *[x.reshape(1, DH) for x in lam_vecs],
      subln_g.reshape(1, 2 * DH))


_QROWS = 8
DECODE_T = 4


def _decode_kernel(pt_ref, ns2_ref, qh_ref, *refs, n_group, n_pages, lam_init):
    k_refs = refs[:2 * n_group]
    v_refs = refs[2 * n_group:4 * n_group]
    (distb_ref, kn_ref, vn_ref, distn_ref, lq1, lk1, lq2, lk2, g_ref, o_ref,
     s_ref, mx_ref, m_ref, inv_ref, acc_ref) = refs[4 * n_group:]
    ph = pl.program_id(1)
    j = pl.program_id(2)
    last = pl.num_programs(2) - 1
    k1 = (DH ** -0.5) * LOG2E
    hw = 2 * DH
    first_comp = lax.broadcasted_iota(jnp.int32, (_QROWS, PAGE_SIZE), 0) < DECODE_T

    def head_rows(ref, h):
        return ref[0, 0, pl.ds(h, PAGE_SIZE, stride=H_B), :].astype(BF16)

    def score(h, k0, k1_, dist):
        s0 = _dot_nt(qh_ref[0, h, 0], k0)
        s1 = _dot_nt(qh_ref[0, h, 1], k1_)
        s = jnp.where(first_comp, s0[:_QROWS], s1[:_QROWS])
        return s * k1 + dist * ns2_ref[h]

    @pl.when((ph == 0) & (j == 0))
    def _():
        mx_ref[...] = jnp.full(mx_ref.shape, -jnp.inf, F32)

    @pl.when(ph == 0)
    def _():
        for g in range(n_group):
            p = j * n_group + g
            dist = distb_ref[...] - (p * PAGE_SIZE).astype(F32)
            for h in range(H_B):
                s8 = score(h, head_rows(k_refs[2 * g], h), head_rows(k_refs[2 * g + 1], h), dist)
                s_ref[p, h] = s8
                mx_ref[h] = jnp.maximum(mx_ref[h], s8)

        @pl.when(j == last)
        def _():
            for h in range(H_B):
                s8 = score(h, kn_ref[0, :, h * hw:h * hw + DH], kn_ref[0, :, h * hw + DH:(h + 1) * hw],
                           distn_ref[...])
                s_ref[n_pages, h] = s8
                mx = jnp.max(jnp.maximum(mx_ref[h], s8), axis=-1, keepdims=True)
                m_ref[h] = jnp.broadcast_to(mx, (_QROWS, PAGE_SIZE))

            def sum_body(p, carry):
                return tuple(carry[h] + jnp.exp2(s_ref[p, h] - m_ref[h]) for h in range(H_B))

            sums = lax.fori_loop(0, n_pages + 1, sum_body,
                                 tuple(jnp.zeros((_QROWS, PAGE_SIZE), F32) for _ in range(H_B)))
            for h in range(H_B):
                l = jnp.sum(sums[h], axis=-1, keepdims=True)
                inv_ref[h] = jnp.broadcast_to(1.0 / l, (_QROWS, PAGE_SIZE))

    @pl.when((ph == 1) & (j == 0))
    def _():
        acc_ref[...] = jnp.zeros(acc_ref.shape, F32)

    @pl.when(ph == 1)
    def _():
        lam = _lam(lq1, lk1, lq2, lk2, lam_init)

        def weights(p, h):
            e = jnp.exp2(s_ref[p, h] - m_ref[h]) * inv_ref[h]
            a = e - lam * pltpu.roll(e, DECODE_T, 0)
            return jnp.concatenate([a, jnp.zeros_like(a)], axis=0).astype(BF16)

        for g in range(n_group):
            p = j * n_group + g
            for h in range(H_B):
                a = weights(p, h)
                for c in range(2):
                    acc_ref[h, :, c * DH:(c + 1) * DH] += _dot(a, head_rows(v_refs[2 * g + c], h))

        @pl.when(j == last)
        def _():
            for h in range(H_B):
                o = acc_ref[h] + _dot(weights(n_pages, h), vn_ref[0, :, h * hw:(h + 1) * hw])
                o_ref[0, :, h * hw:(h + 1) * hw] = _subln(o[:DECODE_T], lam_init, g_ref).astype(o_ref.dtype)


def attn_decode(layer, qb, kb, vb, cache_k4, cache_v4, page_table, lam_vecs, subln_g, lam_init, t):
    assert t == DECODE_T, "decode kernel lays out DECODE_T query tokens per score tile"
    nb, n_pages = page_table.shape
    n_group = math.gcd(DECODE_PAGES, n_pages)
    n_steps = n_pages // n_group
    past = n_pages * PAGE_SIZE
    hw = 2 * DH
    rows = PAGE_SIZE * H_B
    q5 = qb.reshape(nb, t, H_B, 2, DH).transpose(0, 2, 3, 1, 4)
    qh = jnp.stack([jnp.pad(q5[:, :, 0], ((0, 0), (0, 0), (0, 2 * _QROWS - t), (0, 0))),
                    jnp.pad(q5[:, :, 1], ((0, 0), (0, 0), (t, 2 * _QROWS - 2 * t), (0, 0)))], axis=2)
    pad_rows = ((0, 0), (0, PAGE_SIZE - t), (0, 0))
    kn = jnp.pad(kb.reshape(nb, t, D_B), pad_rows)
    vn = jnp.pad(vb.reshape(nb, t, D_B), pad_rows)
    r_tok = (jnp.arange(_QROWS) % t).astype(F32)[:, None]
    pos = jnp.arange(PAGE_SIZE, dtype=F32)[None, :]
    distb = past + r_tok - pos
    distn = jnp.where((pos <= r_tok) & (pos < t), r_tok - pos, jnp.inf)
    ns2 = -_alibi_slopes() * LOG2E
    vec = pl.BlockSpec((1, DH), lambda b, ph, j, pt: (0, 0))
    tile = pl.BlockSpec((_QROWS, PAGE_SIZE), lambda b, ph, j, pt: (0, 0))
    new_page = pl.BlockSpec((1, PAGE_SIZE, D_B), lambda b, ph, j, pt: (b, 0, 0))
    k_pages = [pl.BlockSpec((1, 1, rows, DH),
                            lambda b, ph, j, pt, g=g, c=c:
                            (layer, pt[b, (j + (n_steps - 1 - j) * ph) * n_group + g], 0, c))
               for g in range(n_group) for c in range(2)]
    v_pages = [pl.BlockSpec((1, 1, rows, DH),
                            lambda b, ph, j, pt, g=g, c=c: (layer, pt[b, j * ph * n_group + g], 0, c))
               for g in range(n_group) for c in range(2)]
    grid_spec = pltpu.PrefetchScalarGridSpec(
        num_scalar_prefetch=1,
        grid=(nb, 2, n_steps),
        in_specs=[pl.BlockSpec(memory_space=pltpu.SMEM),
                  pl.BlockSpec((1, H_B, 2, 2 * _QROWS, DH), lambda b, ph, j, pt: (b, 0, 0, 0, 0))]
                 + k_pages + v_pages
                 + [tile, new_page, new_page, tile, vec, vec, vec, vec,
                    pl.BlockSpec((1, hw), lambda b, ph, j, pt: (0, 0))],
        out_specs=pl.BlockSpec((1, t, D_B), lambda b, ph, j, pt: (b, 0, 0)),
        scratch_shapes=[pltpu.VMEM((n_pages + 1, H_B, _QROWS, PAGE_SIZE), F32),
                        pltpu.VMEM((H_B, _QROWS, PAGE_SIZE), F32),
                        pltpu.VMEM((H_B, _QROWS, PAGE_SIZE), F32),
                        pltpu.VMEM((H_B, _QROWS, PAGE_SIZE), F32),
                        pltpu.VMEM((H_B, 2 * _QROWS, hw), F32)],
    )
    out = pl.pallas_call(
        functools.partial(_decode_kernel, n_group=n_group, n_pages=n_pages, lam_init=lam_init),
        grid_spec=grid_spec,
        out_shape=jax.ShapeDtypeStruct((nb, t, D_B), BF16),
        compiler_params=_params(("parallel", "arbitrary", "arbitrary")),
    )(page_table, ns2, qh, *([cache_k4] * (2 * n_group)), *([cache_v4] * (2 * n_group)), distb, kn, vn, distn,
      *[x.reshape(1, DH) for x in lam_vecs], subln_g.reshape(1, hw))
    return out.reshape(nb * t, D_B)


def _merge_kernel(a_ref, b_ref, c_ref, wa_ref, wb_ref, wc_ref, ga_ref, gb_ref, gc_ref, o_ref,
                  wab_ref, wbb_ref, wcb_ref):
    @pl.when(pl.program_id(1) == 0)
    def _():
        wab_ref[...] = wa_ref[0].astype(BF16)
        wbb_ref[...] = wb_ref[0].astype(BF16)
        wcb_ref[...] = wc_ref[0].astype(BF16)

    m = jax.nn.sigmoid(ga_ref[...]) * _dot(a_ref[...], wab_ref[...])
    m = m + jax.nn.sigmoid(gb_ref[...]) * _dot(b_ref[...], wbb_ref[...])
    m = m + jax.nn.sigmoid(gc_ref[...]) * _dot(c_ref[...], wcb_ref[...])
    o_ref[...] = m.astype(o_ref.dtype)


def merge_branches(a, b, c, wa, wb, wc, layer, z2, tm=512, tn=512):
    m = a.shape[0]
    d = wa.shape[2]
    tm = min(tm, m)
    act = lambda k: pl.BlockSpec((tm, k), lambda j, i: (i, 0))
    wgt = lambda k: pl.BlockSpec((1, k, tn), lambda j, i: (layer, 0, j))
    gate = lambda br: pl.BlockSpec((tm, tn), lambda j, i: (i, (OFF_GATE + br * d) // tn + j))
    return pl.pallas_call(
        _merge_kernel,
        grid=(d // tn, m // tm),
        in_specs=[act(D_A), act(D_B), act(D_C), wgt(D_A), wgt(D_B), wgt(D_C),
                  gate(0), gate(1), gate(2)],
        out_specs=pl.BlockSpec((tm, tn), lambda j, i: (i, j)),
        out_shape=jax.ShapeDtypeStruct((m, d), BF16),
        scratch_shapes=[pltpu.VMEM((D_A, tn), BF16), pltpu.VMEM((D_B, tn), BF16),
                        pltpu.VMEM((D_C, tn), BF16)],
        compiler_params=_params(("parallel", "arbitrary"), 52),
    )(a, b, c, wa, wb, wc, z2, z2, z2)


def _router_kernel(x_ref, w_ref, b_ref, o_ref):
    logit = _dot(x_ref[...].astype(BF16), w_ref[...].astype(BF16)) + b_ref[...]
    lane = lax.broadcasted_iota(jnp.int32, logit.shape, 1).astype(F32)
    big = float(LANES)
    ninf = -jnp.inf

    def first_max(x):
        mx = jnp.max(x, axis=-1, keepdims=True)
        return mx, jnp.min(jnp.where(x == mx, lane, big), axis=-1, keepdims=True)

    gl = jnp.where(lane < N_GROUPS, logit, ninf)
    gmax, gidx = first_max(gl)
    g_top = 1.0 / jnp.sum(jnp.exp(gl - gmax), axis=-1, keepdims=True)
    lo = N_GROUPS + N_EXP_PER_GROUP * gidx
    el = jnp.where((lane >= lo) & (lane < lo + N_EXP_PER_GROUP), logit, ninf)
    l1, i1 = first_max(el)
    l2, i2 = first_max(jnp.where(lane == i1, ninf, el))
    e2 = jnp.exp(l2 - l1)
    den = 1.0 + e2
    w1 = g_top / den
    w2 = g_top * e2 / den
    out = jnp.where(lane == 0, i1 - N_GROUPS, 0.0)
    out = jnp.where(lane == 1, i2 - N_GROUPS, out)
    out = jnp.where(lane == 2, w1, out)
    out = jnp.where(lane == 3, w2, out)
    o_ref[...] = out


def router(xt, wg, bg, we, be, rows=None, tr=256):
    d = xt.shape[1]
    m = xt.shape[0] if rows is None else rows
    tr = min(tr, m)
    pad = LANES - N_GROUPS - N_EXPERTS
    w = jnp.concatenate([wg, we, jnp.zeros((d, pad), F32)], axis=1)
    b = jnp.concatenate([bg, be, jnp.zeros((pad,), F32)]).reshape(1, LANES)
    return pl.pallas_call(
        _router_kernel,
        grid=(m // tr,),
        in_specs=[pl.BlockSpec((tr, d), lambda i: (i, 0)),
                  pl.BlockSpec((d, LANES), lambda i: (0, 0)),
                  pl.BlockSpec((1, LANES), lambda i: (0, 0))],
        out_specs=pl.BlockSpec((tr, LANES), lambda i: (i, 0)),
        out_shape=jax.ShapeDtypeStruct((m, LANES), F32),
        compiler_params=_params(("parallel",)),
    )(xt, w, b)


_ROW_UNROLL = 8


def _moe_kernel(te_ref, nv_ref, p0_ref, tok_ref, dst_ref, x_hbm, rw_ref, wg_ref, wu_ref, wd_ref, o_hbm,
                xbuf, obuf, gsem, ssem, *, tm):
    i = pl.program_id(0)
    nt = pl.num_programs(0)
    slot = i % 2

    def start_rows(tile, start_row):
        n = nv_ref[tile]
        base = p0_ref[tile]

        def body(g, carry):
            for u in range(_ROW_UNROLL):
                start_row(base, g * _ROW_UNROLL + u, u % 2)
            return carry

        def tail(r, carry):
            start_row(base, r, 0)
            return carry

        n_full = n // _ROW_UNROLL
        lax.fori_loop(0, n_full, body, 0)
        lax.fori_loop(n_full * _ROW_UNROLL, n, tail, 0)

    def gather_rows(tile, sl):
        def start_row(base, r, priority):
            pltpu.make_async_copy(x_hbm.at[pl.ds(tok_ref[base + r], 1)], xbuf.at[sl, pl.ds(r, 1)],
                                  gsem.at[sl]).start(priority=priority)
        start_rows(tile, start_row)

    def scatter_rows(tile, sl):
        def start_row(base, r, priority):
            pltpu.make_async_copy(obuf.at[sl, pl.ds(r, 1)], o_hbm.at[pl.ds(dst_ref[base + r], 1)],
                                  ssem.at[sl]).start(priority=priority)
        start_rows(tile, start_row)

    def wait_rows(tile, block_copy):
        n = nv_ref[tile]
        k = tm
        while k >= 1:
            @pl.when((n & k) != 0)
            def _(k=k):
                block_copy(k).wait()
            k //= 2

    def gather_block(sl):
        return lambda k: pltpu.make_async_copy(x_hbm.at[pl.ds(0, k)], xbuf.at[sl, pl.ds(0, k)], gsem.at[sl])

    def scatter_block(sl):
        return lambda k: pltpu.make_async_copy(obuf.at[sl, pl.ds(0, k)], o_hbm.at[pl.ds(0, k)], ssem.at[sl])

    @pl.when(i == 0)
    def _():
        xbuf[...] = jnp.zeros(xbuf.shape, xbuf.dtype)
        gather_rows(0, 0)

    @pl.when(i + 1 < nt)
    def _():
        gather_rows(i + 1, 1 - slot)

    @pl.when(i >= 2)
    def _():
        wait_rows(i - 2, scatter_block(slot))

    wait_rows(i, gather_block(slot))

    @pl.when(nv_ref[i] > 0)
    def _():
        x = xbuf[slot].astype(BF16)
        hg = _dot(x, wg_ref[0, 0])
        hu = _dot(x, wu_ref[0, 0])
        hmid = (hg * jax.nn.sigmoid(hg)) * hu * rw_ref[...]
        obuf[slot] = _dot(hmid.astype(BF16), wd_ref[0, 0])

    scatter_rows(i, slot)

    @pl.when(i == nt - 1)
    def _():
        @pl.when(i >= 1)
        def _():
            wait_rows(i - 1, scatter_block(1 - slot))
        wait_rows(i, scatter_block(slot))


def moe_experts(xt, ids, wts, wg, wu, wd, layer, tm=MOE_TM):
    m, d = ids.shape[0], xt.shape[1]
    n_pairs = 2 * m
    nt = -(-n_pairs // tm) + N_EXPERTS
    flat_e = ids.reshape(-1)
    order = jnp.argsort(flat_e, stable=True).astype(jnp.int32)
    sorted_tok = order // 2
    sorted_dst = (order % 2) * m + sorted_tok
    counts = jnp.sum(flat_e[:, None] == jnp.arange(N_EXPERTS, dtype=jnp.int32)[None, :],
                     axis=0, dtype=jnp.int32)
    tiles_e = (counts + tm - 1) // tm
    tile_end = jnp.cumsum(tiles_e)
    tile_start = tile_end - tiles_e
    pair_start = jnp.cumsum(counts) - counts
    tile_idx = jnp.arange(nt, dtype=jnp.int32)
    tile_e = jnp.minimum(jnp.searchsorted(tile_end, tile_idx, side="right"), N_EXPERTS - 1).astype(jnp.int32)
    in_expert = (tile_idx - tile_start[tile_e]) * tm
    n_valid = jnp.clip(counts[tile_e] - in_expert, 0, tm)
    n_valid = jnp.where(tile_idx < tile_end[-1], n_valid, 0).astype(jnp.int32)
    tile_p0 = jnp.where(n_valid > 0, pair_start[tile_e] + in_expert, 0).astype(jnp.int32)
    r = jnp.arange(tm, dtype=jnp.int32)[None, :]
    w_idx = jnp.minimum(tile_p0[:, None] + r, n_pairs - 1)
    row_w = jnp.where(r < n_valid[:, None], wts.reshape(-1)[order][w_idx], 0.0).reshape(nt * tm, 1)
    dff = wg.shape[-1]
    wspec = lambda shape: pl.BlockSpec((1, 1) + shape, lambda i, te, nv, p0, tk, ds: (layer, te[i], 0, 0))
    grid_spec = pltpu.PrefetchScalarGridSpec(
        num_scalar_prefetch=5,
        grid=(nt,),
        in_specs=[pl.BlockSpec(memory_space=pl.ANY),
                  pl.BlockSpec((tm, 1), lambda i, te, nv, p0, tk, ds: (i, 0)),
                  wspec((d, dff)), wspec((d, dff)), wspec((dff, d))],
        out_specs=pl.BlockSpec(memory_space=pl.ANY),
        scratch_shapes=[pltpu.VMEM((2, tm, d), F32), pltpu.VMEM((2, tm, d), F32),
                        pltpu.SemaphoreType.DMA((2,)), pltpu.SemaphoreType.DMA((2,))],
    )
    out = pl.pallas_call(
        functools.partial(_moe_kernel, tm=tm),
        grid_spec=grid_spec,
        out_shape=jax.ShapeDtypeStruct((2 * m, d), F32),
        compiler_params=_params(("arbitrary",), 56),
    )(tile_e, n_valid, tile_p0, sorted_tok, sorted_dst, xt, row_w, wg, wu, wd)
    return out.reshape(2, m, d)


def _combine_kernel(h_ref, o0_ref, o1_ref, y_ref):
    y_ref[...] = h_ref[...] + (o0_ref[0] + o1_ref[0])


def moe_combine(h, o2, row0, tr=256):
    m, d = h.shape
    tr = min(tr, m)
    return pl.pallas_call(
        _combine_kernel,
        grid=(m // tr,),
        in_specs=[pl.BlockSpec((tr, d), lambda i: (i, 0)),
                  pl.BlockSpec((1, tr, d), lambda i: (0, row0 // tr + i, 0)),
                  pl.BlockSpec((1, tr, d), lambda i: (1, row0 // tr + i, 0))],
        out_specs=pl.BlockSpec((tr, d), lambda i: (i, 0)),
        out_shape=jax.ShapeDtypeStruct((m, d), F32),
        compiler_params=_params(("parallel",)),
    )(h, o2, o2)


def kernel(x_prompt, x_sample, cache_k, cache_v, state_conv, page_table, norm1_g, w_in, sgu_norm_g, sgu_w, sgu_b, q_norm_g, k_norm_g, lam_q1, lam_k1, lam_q2, lam_k2, subln_g, conv_w, w_br_a, w_br_b, w_br_c, w_out, norm2_g, router_group_w, router_group_b, router_expert_w, router_expert_b, expert_w_gate, expert_w_up, expert_w_down):
    depth = w_in.shape[0]
    n_p, t_p, d = x_prompt.shape
    n_s, t_s, _ = x_sample.shape
    m_p, m_s = n_p * t_p, n_s * t_s
    n_in = w_in.shape[-1]
    pool = cache_k.shape[1]
    cache_k4 = cache_k.reshape(depth, pool, PAGE_SIZE * H_B, 2 * DH)
    cache_v4 = cache_v.reshape(depth, pool, PAGE_SIZE * H_B, 2 * DH)
    wg_bf, wu_bf, wd_bf = (w.astype(BF16) for w in (expert_w_gate, expert_w_up, expert_w_down))

    xp = x_prompt.reshape(m_p, d)
    xs = x_sample.reshape(m_s, d)
    outs = {k: [] for k in ("kp", "vp", "ks", "vs", "cp", "cs", "ap", "as")}
    for l in range(depth):
        lam_init = 0.8 - 0.6 * math.exp(-0.3 * l)
        lam_vecs = (lam_q1[l], lam_k1[l], lam_q2[l], lam_k2[l])

        def trunk(x2, mixers, tm, xt_tail=None):
            xn = rmsnorm_rows(x2, norm1_g[l], BF16)
            z2 = matmul_wcast(xn, w_in, l, F32, tm=tm, tn=512)
            a_out, b_out, c_out, extras = mixers(z2)
            mm = merge_branches(a_out, b_out, c_out, w_br_a, w_br_b, w_br_c, l, z2)
            hh = matmul_wcast(mm, w_out, l, F32, tm=tm, tn=512, residual=x2)
            if xt_tail is None:
                xt = rmsnorm_rows(hh, norm2_g[l], F32)
            else:
                xt = rmsnorm_rows_with_tail(hh, norm2_g[l], xt_tail)
            route = router(xt, router_group_w[l], router_group_b[l],
                           router_expert_w[l], router_expert_b[l], rows=x2.shape[0])
            return hh, xt, route, extras

        def prompt_mixers(z2):
            z3 = z2.reshape(n_p, t_p, n_in)
            a_out, a_rows = sgu_prompt(z3, sgu_norm_g[l], sgu_w[l], sgu_b[l])
            c_out, c_state = short_conv(z3, jnp.zeros((n_p, CONV_W - 1, D_C), F32), conv_w[l])
            kf, vf, qb, kb, vb = qkv_prep(z2, q_norm_g[l], k_norm_g[l])
            b_out = attn_prompt(qb.reshape(n_p, t_p, D_B), kb.reshape(n_p, t_p, D_B),
                                vb.reshape(n_p, t_p, D_B), lam_vecs, subln_g[l], lam_init)
            return (a_out.reshape(m_p, D_A), b_out.reshape(m_p, D_B), c_out.reshape(m_p, D_C),
                    (kf.reshape(n_p, t_p, H_B, 2 * DH), vf.reshape(n_p, t_p, H_B, 2 * DH), c_state, a_rows))

        def sample_mixers(z2):
            z3 = z2.reshape(n_s, t_s, n_in)
            a_out, a_rows = sgu_sample(z3, sgu_norm_g[l], sgu_w[l], sgu_b[l])
            c_out, c_state = short_conv(z3, state_conv[l], conv_w[l])
            kf, vf, qb, kb, vb = qkv_prep(z2, q_norm_g[l], k_norm_g[l])
            b_out = attn_decode(l, qb, kb, vb, cache_k4, cache_v4, page_table, lam_vecs,
                                subln_g[l], lam_init, t_s)
            return (a_out.reshape(m_s, D_A), b_out, c_out.reshape(m_s, D_C),
                    (kf.reshape(n_s, t_s, H_B, 2 * DH), vf.reshape(n_s, t_s, H_B, 2 * DH), c_state, a_rows))

        h_s, xt_s, route_s, ex_s = trunk(xs, sample_mixers, 1024)
        h_p, xt_all, route_p, ex_p = trunk(xp, prompt_mixers, 1024, xt_tail=xt_s)
        route = jnp.concatenate([route_p[:, :4], route_s[:, :4]], axis=0)
        o2 = moe_experts(xt_all, route[:, :2].astype(jnp.int32), route[:, 2:4], wg_bf, wu_bf, wd_bf, l)
        xp = moe_combine(h_p, o2, 0)
        xs = moe_combine(h_s, o2, m_p)

        for key, val in zip(("kp", "vp", "cp", "ap"), ex_p):
            outs[key].append(val)
        for key, val in zip(("ks", "vs", "cs", "as"), ex_s):
            outs[key].append(val)

    st = lambda key: jnp.stack(outs[key])
    return (xp.reshape(n_p, t_p, d), xs.reshape(n_s, t_s, d), st("kp"), st("vp"), st("ks"), st("vs"),
            st("cp"), st("cs"), st("ap"), st("as"))
```
